```python
import jax, jax.numpy as jnp
from jax import lax
import numpy as np

D_MODEL = 1024
BATCH = 4
SEQ = 4096
DEPTH = 4

GRID_W = 64
CTX_LEN = 256
HEAD_DIM = 64
ATTN_DIM = D_MODEL // 2
N_Q_HEADS = ATTN_DIM // HEAD_DIM
N_KV_HEADS = N_Q_HEADS // 4
Q_PER_KV = N_Q_HEADS // N_KV_HEADS
KV_DIM = N_KV_HEADS * HEAD_DIM
GMLP_DIM = D_MODEL - ATTN_DIM
GMLP_GROUP_DIM = 128
N_GMLP_GROUPS = GMLP_DIM // GMLP_GROUP_DIM
CHUNK = 128
MIX_DIM = ATTN_DIM + GMLP_DIM
IN_DIM = ATTN_DIM + 2 * KV_DIM + 2 * GMLP_DIM
FF_DIM = 4 * D_MODEL
Q_BLOCK = 128
ROPE_THETA = 10000.0
EPS = 1e-6
N_MOD = 6

kernel_name = "hybrid_gmlp_gqa_prefix_dit"


def rmsnorm(x, g):
    xf = x.astype(jnp.float32)
    y = xf * lax.rsqrt(jnp.mean(xf * xf, axis=-1, keepdims=True) + EPS)
    return (y * g.astype(jnp.float32)).astype(x.dtype)


def group_layernorm(v, g):
    vf = v.astype(jnp.float32)
    mu = jnp.mean(vf, axis=-1, keepdims=True)
    var = jnp.mean(jnp.square(vf - mu), axis=-1, keepdims=True)
    return ((vf - mu) * lax.rsqrt(var + EPS) * g.astype(jnp.float32)).astype(v.dtype)


def modulation(cond, w_mod, b_mod):
    m = jax.nn.silu(cond) @ w_mod + b_mod
    return m.reshape(m.shape[:-1] + (N_MOD, D_MODEL))


def modulate(h, shift, scale):
    return h * (1 + scale) + shift


def axial_rope_tables(n_lat):
    rows = n_lat // GRID_W
    row = jnp.repeat(jnp.arange(rows, dtype=jnp.float32), GRID_W)
    col = jnp.tile(jnp.arange(GRID_W, dtype=jnp.float32), rows)
    n_freq = HEAD_DIM // 4
    inv_freq = ROPE_THETA ** (-jnp.arange(n_freq, dtype=jnp.float32) / n_freq)
    ang = jnp.stack([row[:, None] * inv_freq, col[:, None] * inv_freq], axis=1)
    return jnp.cos(ang), jnp.sin(ang)


def apply_rope(x, cos, sin):
    xf = x.astype(jnp.float32).reshape(x.shape[:-1] + (2, 2, HEAD_DIM // 4))
    x1, x2 = xf[..., 0, :], xf[..., 1, :]
    out = jnp.stack([x1 * cos - x2 * sin, x2 * cos + x1 * sin], axis=-2)
    return out.reshape(x.shape).astype(x.dtype)


def q_heads(q):
    b, n, _ = q.shape
    return q.reshape(b, n, N_KV_HEADS, Q_PER_KV, HEAD_DIM).transpose(0, 2, 3, 1, 4)


def kv_heads(k):
    b, n, _ = k.shape
    return k.reshape(b, n, N_KV_HEADS, HEAD_DIM).transpose(0, 2, 1, 3)


def gqa_blocks(q, k, v):
    b, kvh, g, n, dh = q.shape
    nb = n // Q_BLOCK
    qb = jnp.moveaxis(q.reshape(b, kvh, g, nb, Q_BLOCK, dh), 3, 0)

    def one_block(qblk):
        s = jnp.einsum('bkgqd,bkmd->bkgqm', qblk, k).astype(jnp.float32)
        p = jax.nn.softmax(s, axis=-1).astype(v.dtype)
        return jnp.einsum('bkgqm,bkmd->bkgqd', p, v)

    o = lax.map(one_block, qb)
    o = jnp.moveaxis(o, 0, 3).reshape(b, kvh, g, n, dh)
    return o.transpose(0, 3, 1, 2, 4).reshape(b, n, ATTN_DIM)


def chunk_gmlp(u, vg, w_s, b_s, g_norm):
    b, n, _ = u.shape
    u = jax.nn.gelu(u)
    vg = group_layernorm(jax.nn.gelu(vg).reshape(b, n, N_GMLP_GROUPS, GMLP_GROUP_DIM),
                         g_norm.reshape(N_GMLP_GROUPS, GMLP_GROUP_DIM))
    vc = vg.reshape(b, n // CHUNK, CHUNK, N_GMLP_GROUPS, GMLP_GROUP_DIM)
    mixed = jnp.einsum('gpq,bnqgc->bnpgc', w_s, vc) + b_s.T[None, None, :, :, None]
    return u * mixed.reshape(b, n, GMLP_DIM)


def sqrelu_mlp(h, w1, w2):
    return jnp.square(jax.nn.relu(h @ w1)) @ w2


def setup_inputs(seed: int = 0) -> dict:
    key = jax.random.key(seed)
    ks = jax.random.split(key, 17)
    f32 = jnp.float32
    nrm = lambda k, shape, s: (jax.random.normal(k, shape, f32) * s)
    gain = lambda k, shape: 1.0 + 0.02 * jax.random.normal(k, shape, f32)
    return {
        "x": nrm(ks[0], (BATCH, SEQ, D_MODEL), 1.0),
        "c": nrm(ks[1], (BATCH, D_MODEL), 1.0),
        "ctx": nrm(ks[2], (BATCH, CTX_LEN, D_MODEL), 1.0),
        "c_ctx": nrm(ks[3], (D_MODEL,), 1.0),
        "w_mod": nrm(ks[4], (DEPTH, D_MODEL, N_MOD * D_MODEL), D_MODEL ** -0.5),
        "b_mod": nrm(ks[5], (DEPTH, N_MOD * D_MODEL), 0.02),
        "norm1_g": gain(ks[6], (DEPTH, D_MODEL)),
        "w_in": nrm(ks[7], (DEPTH, D_MODEL, IN_DIM), D_MODEL ** -0.5),
        "q_norm_g": gain(ks[8], (DEPTH, HEAD_DIM)),
        "k_norm_g": gain(ks[9], (DEPTH, HEAD_DIM)),
        "gmlp_norm_g": gain(ks[10], (DEPTH, GMLP_DIM)),
        "w_spatial": nrm(ks[11], (DEPTH, N_GMLP_GROUPS, CHUNK, CHUNK), CHUNK ** -0.5),
        "b_spatial": gain(ks[12], (DEPTH, N_GMLP_GROUPS, CHUNK)),
        "w_out": nrm(ks[13], (DEPTH, MIX_DIM, D_MODEL), MIX_DIM ** -0.5),
        "norm2_g": gain(ks[14], (DEPTH, D_MODEL)),
        "w_ff1": nrm(ks[15], (DEPTH, D_MODEL, FF_DIM), D_MODEL ** -0.5),
        "w_ff2": nrm(ks[16], (DEPTH, FF_DIM, D_MODEL), FF_DIM ** -0.5),
    }


def reference(x, c, ctx, c_ctx, w_mod, b_mod, norm1_g, w_in, q_norm_g, k_norm_g, gmlp_norm_g,
              w_spatial, b_spatial, w_out, norm2_g, w_ff1, w_ff2):
    n_lat = x.shape[1]
    cos, sin = axial_rope_tables(n_lat)
    q_scale = HEAD_DIM ** -0.5
    kv_lo, kv_hi = ATTN_DIM, ATTN_DIM + 2 * KV_DIM
    x_lat, x_ctx = x, ctx
    for l in range(DEPTH):
        last = l == DEPTH - 1
        m_lat = modulation(c, w_mod[l], b_mod[l])
        m_ctx = modulation(c_ctx, w_mod[l], b_mod[l])
        sh1, sc1, ga1, sh2, sc2, ga2 = [m_lat[:, i, None, :] for i in range(N_MOD)]
        csh1, csc1, cga1, csh2, csc2, cga2 = [m_ctx[None, i, None, :] for i in range(N_MOD)]

        h_lat = modulate(rmsnorm(x_lat, norm1_g[l]), sh1, sc1)
        h_ctx = modulate(rmsnorm(x_ctx, norm1_g[l]), csh1, csc1)
        z_lat = h_lat @ w_in[l]
        q_l = z_lat[..., :ATTN_DIM]
        k_l = z_lat[..., kv_lo:kv_lo + KV_DIM]
        v_l = z_lat[..., kv_lo + KV_DIM:kv_hi]
        u_l = z_lat[..., kv_hi:kv_hi + GMLP_DIM]
        g_l = z_lat[..., kv_hi + GMLP_DIM:]
        if last:
            z_ctx = h_ctx @ w_in[l][:, kv_lo:kv_hi]
            k_c, v_c = z_ctx[..., :KV_DIM], z_ctx[..., KV_DIM:]
        else:
            z_ctx = h_ctx @ w_in[l]
            q_c = z_ctx[..., :ATTN_DIM]
            k_c = z_ctx[..., kv_lo:kv_lo + KV_DIM]
            v_c = z_ctx[..., kv_lo + KV_DIM:kv_hi]
            u_c = z_ctx[..., kv_hi:kv_hi + GMLP_DIM]
            g_c = z_ctx[..., kv_hi + GMLP_DIM:]

        qh_l = apply_rope(rmsnorm(q_heads(q_l), q_norm_g[l]), cos, sin) * q_scale
        kh_l = apply_rope(rmsnorm(kv_heads(k_l), k_norm_g[l]), cos, sin)
        kh_c = rmsnorm(kv_heads(k_c), k_norm_g[l])
        vh_c = kv_heads(v_c)
        k_all = jnp.concatenate([kh_c, kh_l], axis=2)
        v_all = jnp.concatenate([vh_c, kv_heads(v_l)], axis=2)
        attn_lat = gqa_blocks(qh_l, k_all, v_all)
        gm_lat = chunk_gmlp(u_l, g_l, w_spatial[l], b_spatial[l], gmlp_norm_g[l])
        x_lat = x_lat + ga1 * (jnp.concatenate([attn_lat, gm_lat], axis=-1) @ w_out[l])
        x_lat = x_lat + ga2 * sqrelu_mlp(modulate(rmsnorm(x_lat, norm2_g[l]), sh2, sc2),
                                         w_ff1[l], w_ff2[l])

        if not last:
            qh_c = rmsnorm(q_heads(q_c), q_norm_g[l]) * q_scale
            attn_ctx = gqa_blocks(qh_c, kh_c, vh_c)
            gm_ctx = chunk_gmlp(u_c, g_c, w_spatial[l], b_spatial[l], gmlp_norm_g[l])
            x_ctx = x_ctx + cga1 * (jnp.concatenate([attn_ctx, gm_ctx], axis=-1) @ w_out[l])
            x_ctx = x_ctx + cga2 * sqrelu_mlp(modulate(rmsnorm(x_ctx, norm2_g[l]), csh2, csc2),
                                              w_ff1[l], w_ff2[l])
    return x_lat
```

```python
import functools

import numpy as np
import jax
import jax.numpy as jnp
from jax import lax
from jax.experimental import pallas as pl
from jax.experimental.pallas import tpu as pltpu

F32 = jnp.float32
BF16 = jnp.bfloat16

HEAD_DIM = 64
Q_PER_KV = 4
N_KV_HEADS = 2
KV_DIM = N_KV_HEADS * HEAD_DIM
GROUP_DIM = 128
N_GROUPS = 4
GRID_W = 64
ROPE_THETA = 10000.0
EPS = 1e-6
N_MOD = 6
Q_SCALE = HEAD_DIM ** -0.5

LANES = 128
ROW_TILE = 512
KEY_TILE = 256
Q_TILE = 128
MOD_COL_TILE = 1536
MOD_ROWS = 8
VMEM_LIMIT = 56 * 1024 * 1024


def _dot(a, b):
    return jnp.dot(a, b, preferred_element_type=F32)


def _split_bf16(t):
    hi = t.astype(BF16)
    lo = (t - hi.astype(F32)).astype(BF16)
    return hi, lo


def _params(*sem):
    return pltpu.CompilerParams(dimension_semantics=sem, vmem_limit_bytes=VMEM_LIMIT)


def _mod_kernel(cond_ref, w_ref, b_ref, o_ref):
    cnd = cond_ref[...]
    act = cnd * (1.0 / (1.0 + jnp.exp(-cnd)))
    a_hi, a_lo = _split_bf16(act)
    w_hi, w_lo = _split_bf16(w_ref[...])
    acc = _dot(a_hi, w_hi) + _dot(a_lo, w_hi) + _dot(a_hi, w_lo)
    o_ref[...] = acc + b_ref[...]


def _modulation(cond, w_mod, b_mod):
    depth, d, n_out = w_mod.shape
    return pl.pallas_call(
        _mod_kernel,
        grid=(depth, n_out // MOD_COL_TILE),
        in_specs=[
            pl.BlockSpec((MOD_ROWS, d), lambda l, j: (0, 0)),
            pl.BlockSpec((None, d, MOD_COL_TILE), lambda l, j: (l, 0, j)),
            pl.BlockSpec((None, 1, MOD_COL_TILE), lambda l, j: (l, 0, j)),
        ],
        out_specs=pl.BlockSpec((None, MOD_ROWS, MOD_COL_TILE), lambda l, j: (l, 0, j)),
        out_shape=jax.ShapeDtypeStruct((depth, MOD_ROWS, n_out), F32),
        compiler_params=_params("arbitrary", "arbitrary"),
        name="modulation",
    )(cond, w_mod, b_mod.reshape(depth, 1, n_out))


def _inproj_kernel(x_ref, mod_ref, g1_ref, w_ref, qg_ref, kg_ref, cos_ref, sin_ref, s_q_ref, s_k_ref,
                   gng_ref, ws_ref, bs_ref, q_ref, kt_ref, v_ref, gm_ref, *, d_model, attn_dim, gmlp_dim):
    tm = x_ref.shape[0]
    x = x_ref[...]
    mod = mod_ref[...]
    shift, scale = mod[:, 0:d_model], mod[:, d_model:2 * d_model]
    ms = jnp.mean(x * x, axis=-1, keepdims=True)
    h = (x * lax.rsqrt(ms + EPS)) * g1_ref[...]
    hb = (h * (1.0 + scale) + shift).astype(BF16)

    cos, sin = cos_ref[...], sin_ref[...]
    lane = lax.broadcasted_iota(jnp.int32, (tm, LANES), 1)
    second_half = (lane & (HEAD_DIM // 4)) != 0

    def rope(t):
        partner = jnp.where(second_half, pltpu.roll(t, HEAD_DIM // 4, 1),
                            pltpu.roll(t, LANES - HEAD_DIM // 4, 1))
        return t * cos + partner * sin

    def head_mean_sq(t, s_ref):
        hi, lo = _split_bf16(t * t)
        return (_dot(hi, s_ref[...]) + _dot(lo, s_ref[...])) * (1.0 / HEAD_DIM)

    kv_lo = attn_dim
    u_lo = attn_dim + 2 * KV_DIM
    g_lo = u_lo + gmlp_dim

    zq = _dot(hb, w_ref[:, 0:attn_dim])
    qn = (zq * lax.rsqrt(head_mean_sq(zq, s_q_ref) + EPS)) * qg_ref[...]
    for j in range(attn_dim // LANES):
        cols = slice(j * LANES, (j + 1) * LANES)
        q_ref[:, cols] = (rope(qn[:, cols]) * Q_SCALE).astype(BF16)

    zkv = _dot(hb, w_ref[:, kv_lo:u_lo])
    zk, zv = zkv[:, 0:KV_DIM], zkv[:, KV_DIM:2 * KV_DIM]
    kn = (zk * lax.rsqrt(head_mean_sq(zk, s_k_ref) + EPS)) * kg_ref[...]
    kt = rope(kn).T
    for c in range(tm // KEY_TILE):
        kt_ref[c] = kt[:, c * KEY_TILE:(c + 1) * KEY_TILE].astype(BF16)
    v_ref[...] = zv.astype(BF16)

    u = jax.nn.gelu(_dot(hb, w_ref[:, u_lo:g_lo]), approximate=True)
    gate = jax.nn.gelu(_dot(hb, w_ref[:, g_lo:g_lo + gmlp_dim]), approximate=True)
    for g in range(N_GROUPS):
        cols = slice(g * GROUP_DIM, (g + 1) * GROUP_DIM)
        t = gate[:, cols]
        dev = t - jnp.mean(t, axis=-1, keepdims=True)
        var = jnp.mean(dev * dev, axis=-1, keepdims=True)
        vn = ((dev * lax.rsqrt(var + EPS)) * gng_ref[:, cols]).astype(BF16)
        for c in range(tm // GROUP_DIM):
            rows = slice(c * GROUP_DIM, (c + 1) * GROUP_DIM)
            mixed = _dot(ws_ref[g], vn[rows, :]) + bs_ref[:, g:g + 1]
            gm_ref[rows, cols] = (u[rows, cols] * mixed).astype(BF16)


def _inproj(layer, xs, mods, g1, w_in, qg, kg, cos_t, sin_t, s_q, s_k, gng, ws, bs, *, lat_tiles, tiles_per_batch):
    rows, d = xs.shape
    in_dim = w_in.shape[-1]
    attn_dim = qg.shape[-1]
    gmlp_dim = gng.shape[-1]
    n_tiles = rows // ROW_TILE
    ctx_mod_row = lat_tiles // tiles_per_batch
    ident_tile = cos_t.shape[0] // ROW_TILE - 1

    def mod_idx(j):
        return (layer, jnp.where(j < lat_tiles, j // tiles_per_batch, ctx_mod_row), 0, 0)

    def rope_idx(j):
        return (jnp.where(j < lat_tiles, j % tiles_per_batch, ident_tile), 0)

    const2 = lambda j: (0, 0)
    kern = functools.partial(_inproj_kernel, d_model=d, attn_dim=attn_dim, gmlp_dim=gmlp_dim)
    return pl.pallas_call(
        kern,
        grid=(n_tiles,),
        in_specs=[
            pl.BlockSpec((ROW_TILE, d), lambda j: (j, 0)),
            pl.BlockSpec((None, None, 1, mods.shape[-1]), mod_idx),
            pl.BlockSpec((1, d), const2),
            pl.BlockSpec((None, d, in_dim), lambda j: (layer, 0, 0)),
            pl.BlockSpec((1, attn_dim), const2),
            pl.BlockSpec((1, KV_DIM), const2),
            pl.BlockSpec((ROW_TILE, LANES), rope_idx),
            pl.BlockSpec((ROW_TILE, LANES), rope_idx),
            pl.BlockSpec(s_q.shape, const2),
            pl.BlockSpec(s_k.shape, const2),
            pl.BlockSpec((1, gmlp_dim), const2),
            pl.BlockSpec(ws.shape, lambda j: (0, 0, 0)),
            pl.BlockSpec(bs.shape, const2),
        ],
        out_specs=[
            pl.BlockSpec((ROW_TILE, attn_dim), lambda j: (j, 0)),
            pl.BlockSpec((ROW_TILE // KEY_TILE, KV_DIM, KEY_TILE), lambda j: (j, 0, 0)),
            pl.BlockSpec((ROW_TILE, KV_DIM), lambda j: (j, 0)),
            pl.BlockSpec((ROW_TILE, gmlp_dim), lambda j: (j, 0)),
        ],
        out_shape=[
            jax.ShapeDtypeStruct((rows, attn_dim), BF16),
            jax.ShapeDtypeStruct((rows // KEY_TILE, KV_DIM, KEY_TILE), BF16),
            jax.ShapeDtypeStruct((rows, KV_DIM), BF16),
            jax.ShapeDtypeStruct((rows, gmlp_dim), BF16),
        ],
        compiler_params=_params("arbitrary"),
        name="inproj",
    )(xs, mods, g1, w_in, qg, kg, cos_t, sin_t, s_q, s_k, gng, ws, bs)


def _attn_kernel(q_ref, ktl_ref, ktc_ref, vl_ref, vc_ref, o_ref, *, lat_q_tiles, has_ctx_queries):
    tq = q_ref.shape[0]
    group_w = Q_PER_KV * HEAD_DIM
    n_steps = ktl_ref.shape[0] // 2

    def stacked_q(kvh):
        qf = q_ref[:, kvh * group_w:(kvh + 1) * group_w].astype(F32)
        heads = [qf[:, h * HEAD_DIM:(h + 1) * HEAD_DIM] for h in range(Q_PER_KV)]
        return jnp.concatenate(heads, axis=0).astype(BF16)

    def ctx_keys(kvh, qs):
        s = _dot(qs, ktc_ref[0, kvh * HEAD_DIM:(kvh + 1) * HEAD_DIM, :])
        m = jnp.max(s, axis=-1, keepdims=True)
        p = jnp.exp(s - m)
        l = jnp.sum(p, axis=-1, keepdims=True)
        return m, l, _dot(p.astype(BF16), vc_ref[...])

    def lat_keys(kvh, qs, carry):
        def step(j, carry):
            m, l, acc = carry
            rows = slice(kvh * HEAD_DIM, (kvh + 1) * HEAD_DIM)
            sa = _dot(qs, ktl_ref[2 * j, rows, :])
            sb = _dot(qs, ktl_ref[2 * j + 1, rows, :])
            m_new = jnp.maximum(m, jnp.maximum(jnp.max(sa, axis=-1, keepdims=True),
                                               jnp.max(sb, axis=-1, keepdims=True)))
            alpha = jnp.exp(m - m_new)
            pa = jnp.exp(sa - m_new)
            pb = jnp.exp(sb - m_new)
            l = alpha * l + (jnp.sum(pa, axis=-1, keepdims=True) + jnp.sum(pb, axis=-1, keepdims=True))
            base = pl.multiple_of(j * (2 * KEY_TILE), 2 * KEY_TILE)
            pv = (_dot(pa.astype(BF16), vl_ref[pl.ds(base, KEY_TILE), :])
                  + _dot(pb.astype(BF16), vl_ref[pl.ds(base + KEY_TILE, KEY_TILE), :]))
            return m_new, l, alpha * acc + pv
        return lax.fori_loop(0, n_steps, step, carry)

    def finish(kvh, l, acc):
        o = acc[:, kvh * HEAD_DIM:(kvh + 1) * HEAD_DIM] / l
        o = jnp.concatenate([o[h * tq:(h + 1) * tq, :] for h in range(Q_PER_KV)], axis=1)
        o_ref[:, kvh * group_w:(kvh + 1) * group_w] = o.astype(BF16)

    def latent_queries():
        for kvh in range(N_KV_HEADS):
            qs = stacked_q(kvh)
            _, l, acc = lat_keys(kvh, qs, ctx_keys(kvh, qs))
            finish(kvh, l, acc)

    def context_queries():
        for kvh in range(N_KV_HEADS):
            _, l, acc = ctx_keys(kvh, stacked_q(kvh))
            finish(kvh, l, acc)

    if has_ctx_queries:
        i = pl.program_id(1)
        pl.when(i < lat_q_tiles)(latent_queries)
        pl.when(i >= lat_q_tiles)(context_queries)
    else:
        latent_queries()


def _attention(q, kt, v, *, batch, seq, ctx_len, has_ctx_queries):
    rows, attn_dim = q.shape
    lat_rows = batch * seq
    lat_q_tiles = seq // Q_TILE
    ctx_q_tiles = ctx_len // Q_TILE
    n_q_tiles = lat_q_tiles + (ctx_q_tiles if has_ctx_queries else 0)
    out_rows = rows if has_ctx_queries else lat_rows

    def q_idx(b, i):
        lat = b * lat_q_tiles + i
        ctx = lat_rows // Q_TILE + b * ctx_q_tiles + (i - lat_q_tiles)
        return (jnp.where(i < lat_q_tiles, lat, ctx), 0)

    kern = functools.partial(_attn_kernel, lat_q_tiles=lat_q_tiles, has_ctx_queries=has_ctx_queries)
    return pl.pallas_call(
        kern,
        grid=(batch, n_q_tiles),
        in_specs=[
            pl.BlockSpec((Q_TILE, attn_dim), q_idx),
            pl.BlockSpec((seq // KEY_TILE, KV_DIM, KEY_TILE), lambda b, i: (b, 0, 0)),
            pl.BlockSpec((ctx_len // KEY_TILE, KV_DIM, KEY_TILE),
                         lambda b, i: (lat_rows // ctx_len + b, 0, 0)),
            pl.BlockSpec((seq, KV_DIM), lambda b, i: (b, 0)),
            pl.BlockSpec((ctx_len, KV_DIM), lambda b, i: (lat_rows // ctx_len + b, 0)),
        ],
        out_specs=pl.BlockSpec((Q_TILE, attn_dim), q_idx),
        out_shape=jax.ShapeDtypeStruct((out_rows, attn_dim), BF16),
        compiler_params=_params("arbitrary", "arbitrary"),
        name="attention",
    )(q, kt, kt, v, v)


def _out_ffn_kernel(x_ref, at_ref, gm_ref, mod_ref, g2_ref, wo_ref, w1_ref, w2_ref, o_ref, *, d_model, ff_tile):
    mod = mod_ref[...]
    gate1 = mod[:, 2 * d_model:3 * d_model]
    shift = mod[:, 3 * d_model:4 * d_model]
    scale = mod[:, 4 * d_model:5 * d_model]
    gate2 = mod[:, 5 * d_model:6 * d_model]
    attn_dim = at_ref.shape[1]
    y = _dot(at_ref[...], wo_ref[0:attn_dim, :]) + _dot(gm_ref[...], wo_ref[attn_dim:, :])
    x1 = x_ref[...] + gate1 * y
    ms = jnp.mean(x1 * x1, axis=-1, keepdims=True)
    h = (x1 * lax.rsqrt(ms + EPS)) * g2_ref[...]
    hb = (h * (1.0 + scale) + shift).astype(BF16)
    acc = None
    for c in range(w1_ref.shape[1] // ff_tile):
        cols = slice(c * ff_tile, (c + 1) * ff_tile)
        t = jnp.maximum(_dot(hb, w1_ref[:, cols]), 0.0)
        part = _dot((t * t).astype(BF16), w2_ref[cols, :])
        acc = part if acc is None else acc + part
    o_ref[...] = x1 + gate2 * acc


def _out_ffn(layer, xs, attn, gm, mods, g2, w_out, w_ff1, w_ff2, *, n_tiles, lat_tiles, tiles_per_batch):
    d = xs.shape[1]
    ff = w_ff1.shape[-1]
    ctx_mod_row = lat_tiles // tiles_per_batch

    def mod_idx(j):
        return (layer, jnp.where(j < lat_tiles, j // tiles_per_batch, ctx_mod_row), 0, 0)

    resident = pl.Buffered(1)
    kern = functools.partial(_out_ffn_kernel, d_model=d, ff_tile=1024)
    return pl.pallas_call(
        kern,
        grid=(n_tiles,),
        in_specs=[
            pl.BlockSpec((ROW_TILE, d), lambda j: (j, 0)),
            pl.BlockSpec((ROW_TILE, attn.shape[1]), lambda j: (j, 0)),
            pl.BlockSpec((ROW_TILE, gm.shape[1]), lambda j: (j, 0)),
            pl.BlockSpec((None, None, 1, mods.shape[-1]), mod_idx),
            pl.BlockSpec((1, d), lambda j: (0, 0)),
            pl.BlockSpec((None,) + w_out.shape[1:], lambda j: (layer, 0, 0), pipeline_mode=resident),
            pl.BlockSpec((None, d, ff), lambda j: (layer, 0, 0), pipeline_mode=resident),
            pl.BlockSpec((None, ff, d), lambda j: (layer, 0, 0), pipeline_mode=resident),
        ],
        out_specs=pl.BlockSpec((ROW_TILE, d), lambda j: (j, 0)),
        out_shape=jax.ShapeDtypeStruct((n_tiles * ROW_TILE, d), F32),
        compiler_params=_params("arbitrary"),
        name="out_ffn",
    )(xs, attn, gm, mods, g2, w_out, w_ff1, w_ff2)


def _rope_tables(seq):
    t = jnp.arange(seq, dtype=jnp.int32)
    row = (t // GRID_W).astype(F32)
    col = (t % GRID_W).astype(F32)
    n_freq = HEAD_DIM // 4
    inv_freq = ROPE_THETA ** (-jnp.arange(n_freq, dtype=F32) / n_freq)
    lane = np.arange(LANES)
    in_head = lane % HEAD_DIM
    axis = in_head // (HEAD_DIM // 2)
    freq = in_head % n_freq
    sign = np.where((in_head // n_freq) % 2 == 1, 1.0, -1.0).astype(np.float32)
    pos = jnp.where(jnp.asarray(axis)[None, :] == 0, row[:, None], col[:, None])
    ang = pos * inv_freq[jnp.asarray(freq)][None, :]
    cos_t = jnp.concatenate([jnp.cos(ang), jnp.ones((ROW_TILE, LANES), F32)], axis=0)
    sin_t = jnp.concatenate([jnp.sin(ang) * jnp.asarray(sign)[None, :], jnp.zeros((ROW_TILE, LANES), F32)], axis=0)
    return cos_t, sin_t


def _head_indicator(width):
    idx = np.arange(width) // HEAD_DIM
    return jnp.asarray((idx[:, None] == idx[None, :]).astype(np.float32), dtype=BF16)


def kernel(x, c, ctx, c_ctx, w_mod, b_mod, norm1_g, w_in, q_norm_g, k_norm_g, gmlp_norm_g, w_spatial, b_spatial,
           w_out, norm2_g, w_ff1, w_ff2):
    batch, seq, d = x.shape
    ctx_len = ctx.shape[1]
    depth = w_mod.shape[0]
    gmlp_dim = gmlp_norm_g.shape[-1]
    attn_dim = w_out.shape[1] - gmlp_dim
    lat_rows = batch * seq
    lat_tiles = lat_rows // ROW_TILE
    tiles_per_batch = seq // ROW_TILE

    cond = jnp.concatenate([c, c_ctx[None, :], jnp.zeros((MOD_ROWS - batch - 1, d), F32)], axis=0)
    mods = _modulation(cond, w_mod, b_mod).reshape(depth, MOD_ROWS, 1, N_MOD * d)

    cos_t, sin_t = _rope_tables(seq)
    s_q = _head_indicator(attn_dim)
    s_k = _head_indicator(KV_DIM)
    w_in_b, w_out_b = w_in.astype(BF16), w_out.astype(BF16)
    w_ff1_b, w_ff2_b = w_ff1.astype(BF16), w_ff2.astype(BF16)
    w_sp_b = w_spatial.astype(BF16)

    xs = jnp.concatenate([x.reshape(lat_rows, d), ctx.reshape(batch * ctx_len, d)], axis=0)
    for l in range(depth):
        last = l == depth - 1
        q, kt, v, gm = _inproj(
            l, xs, mods, norm1_g[l][None, :], w_in_b,
            jnp.tile(q_norm_g[l], attn_dim // HEAD_DIM)[None, :],
            jnp.tile(k_norm_g[l], KV_DIM // HEAD_DIM)[None, :],
            cos_t, sin_t, s_q, s_k, gmlp_norm_g[l][None, :], w_sp_b[l], b_spatial[l].T,
            lat_tiles=lat_tiles, tiles_per_batch=tiles_per_batch)
        attn = _attention(q, kt, v, batch=batch, seq=seq, ctx_len=ctx_len, has_ctx_queries=not last)
        n_tiles = lat_tiles if last else xs.shape[0] // ROW_TILE
        xs = _out_ffn(l, xs, attn, gm, mods, norm2_g[l][None, :], w_out_b, w_ff1_b, w_ff2_b,
                      n_tiles=n_tiles, lat_tiles=lat_tiles, tiles_per_batch=tiles_per_batch)
    return xs.reshape(batch, seq, d)
```

```python
import functools

import numpy as np
import jax
import jax.numpy as jnp
from jax import lax
from jax.experimental import pallas as pl
from jax.experimental.pallas import tpu as pltpu

F32 = jnp.float32
BF16 = jnp.bfloat16

HEAD_DIM = 64
Q_PER_KV = 4
N_KV_HEADS = 2
KV_DIM = N_KV_HEADS * HEAD_DIM
GROUP_DIM = 128
N_GROUPS = 4
GRID_W = 64
ROPE_THETA = 10000.0
EPS = 1e-6
N_MOD = 6
Q_SCALE = HEAD_DIM ** -0.5
LOG2_E = 1.4426950408889634

LANES = 128
ROW_TILE = 512
KEY_TILE = 256
SOFTMAX_COLS = 256
Q_TILE = 128
MOD_COL_TILE = 1536
MOD_ROWS = 8
VMEM_LIMIT = 56 * 1024 * 1024


def _dot(a, b):
    return jnp.dot(a, b, preferred_element_type=F32)


def _split_bf16(t):
    hi = t.astype(BF16)
    lo = (t - hi.astype(F32)).astype(BF16)
    return hi, lo


def _params(*sem):
    return pltpu.CompilerParams(dimension_semantics=sem, vmem_limit_bytes=VMEM_LIMIT)


def _mod_kernel(cond_ref, w_ref, b_ref, o_ref):
    cnd = cond_ref[...]
    act = cnd * (1.0 / (1.0 + jnp.exp(-cnd)))
    a_hi, a_lo = _split_bf16(act)
    w_hi, w_lo = _split_bf16(w_ref[...])
    acc = _dot(a_hi, w_hi) + _dot(a_lo, w_hi) + _dot(a_hi, w_lo)
    o_ref[...] = acc + b_ref[...]


def _modulation(cond, w_mod, b_mod):
    depth, d, n_out = w_mod.shape
    return pl.pallas_call(
        _mod_kernel,
        grid=(depth, n_out // MOD_COL_TILE),
        in_specs=[
            pl.BlockSpec((MOD_ROWS, d), lambda l, j: (0, 0)),
            pl.BlockSpec((None, d, MOD_COL_TILE), lambda l, j: (l, 0, j)),
            pl.BlockSpec((None, 1, MOD_COL_TILE), lambda l, j: (l, 0, j)),
        ],
        out_specs=pl.BlockSpec((None, MOD_ROWS, MOD_COL_TILE), lambda l, j: (l, 0, j)),
        out_shape=jax.ShapeDtypeStruct((depth, MOD_ROWS, n_out), F32),
        compiler_params=_params("arbitrary", "arbitrary"),
        name="modulation",
    )(cond, w_mod, b_mod.reshape(depth, 1, n_out))


def _inproj_kernel(x_ref, mod_ref, g1_ref, w_ref, qg_ref, kg_ref, cos_ref, sin_ref, s_q_ref, s_k_ref,
                   gng_ref, ws_ref, bs_ref, q_ref, k_ref, vt_ref, gm_ref, *, d_model, attn_dim, gmlp_dim):
    tm = x_ref.shape[0]
    x = x_ref[...]
    mod = mod_ref[...]
    shift, scale = mod[:, 0:d_model], mod[:, d_model:2 * d_model]
    ms = jnp.mean(x * x, axis=-1, keepdims=True)
    h = (x * lax.rsqrt(ms + EPS)) * g1_ref[...]
    hb = (h * (1.0 + scale) + shift).astype(BF16)

    cos, sin = cos_ref[...], sin_ref[...]
    lane = lax.broadcasted_iota(jnp.int32, (tm, LANES), 1)
    second_half = (lane & (HEAD_DIM // 4)) != 0

    def rope(t):
        partner = jnp.where(second_half, pltpu.roll(t, HEAD_DIM // 4, 1),
                            pltpu.roll(t, LANES - HEAD_DIM // 4, 1))
        return t * cos + partner * sin

    def head_mean_sq(t, s_ref):
        hi, lo = _split_bf16(t * t)
        return (_dot(hi, s_ref[...]) + _dot(lo, s_ref[...])) * (1.0 / HEAD_DIM)

    kv_lo = attn_dim
    u_lo = attn_dim + 2 * KV_DIM
    g_lo = u_lo + gmlp_dim

    zq = _dot(hb, w_ref[:, 0:attn_dim])
    qn = (zq * lax.rsqrt(head_mean_sq(zq, s_q_ref) + EPS)) * qg_ref[...]
    for j in range(attn_dim // LANES):
        cols = slice(j * LANES, (j + 1) * LANES)
        q_ref[:, cols] = (rope(qn[:, cols]) * (Q_SCALE * LOG2_E)).astype(BF16)

    zkv = _dot(hb, w_ref[:, kv_lo:u_lo])
    zk, zv = zkv[:, 0:KV_DIM], zkv[:, KV_DIM:2 * KV_DIM]
    kn = (zk * lax.rsqrt(head_mean_sq(zk, s_k_ref) + EPS)) * kg_ref[...]
    k_ref[...] = rope(kn).astype(BF16)
    vt = zv.T
    for c in range(tm // KEY_TILE):
        vt_ref[c] = vt[:, c * KEY_TILE:(c + 1) * KEY_TILE].astype(BF16)

    u = jax.nn.gelu(_dot(hb, w_ref[:, u_lo:g_lo]), approximate=True)
    gate = jax.nn.gelu(_dot(hb, w_ref[:, g_lo:g_lo + gmlp_dim]), approximate=True)
    for g in range(N_GROUPS):
        cols = slice(g * GROUP_DIM, (g + 1) * GROUP_DIM)
        t = gate[:, cols]
        dev = t - jnp.mean(t, axis=-1, keepdims=True)
        var = jnp.mean(dev * dev, axis=-1, keepdims=True)
        vn = ((dev * lax.rsqrt(var + EPS)) * gng_ref[:, cols]).astype(BF16)
        for c in range(tm // GROUP_DIM):
            rows = slice(c * GROUP_DIM, (c + 1) * GROUP_DIM)
            mixed = _dot(ws_ref[g], vn[rows, :]) + bs_ref[:, g:g + 1]
            gm_ref[rows, cols] = (u[rows, cols] * mixed).astype(BF16)


def _inproj(layer, xs, mods, g1, w_in, qg, kg, cos_t, sin_t, s_q, s_k, gng, ws, bs, *, lat_tiles, tiles_per_batch):
    rows, d = xs.shape
    in_dim = w_in.shape[-1]
    attn_dim = qg.shape[-1]
    gmlp_dim = gng.shape[-1]
    n_tiles = rows // ROW_TILE
    ctx_mod_row = lat_tiles // tiles_per_batch
    ident_tile = cos_t.shape[0] // ROW_TILE - 1

    def mod_idx(j):
        return (layer, jnp.where(j < lat_tiles, j // tiles_per_batch, ctx_mod_row), 0, 0)

    def rope_idx(j):
        return (jnp.where(j < lat_tiles, j % tiles_per_batch, ident_tile), 0)

    const2 = lambda j: (0, 0)
    kern = functools.partial(_inproj_kernel, d_model=d, attn_dim=attn_dim, gmlp_dim=gmlp_dim)
    return pl.pallas_call(
        kern,
        grid=(n_tiles,),
        in_specs=[
            pl.BlockSpec((ROW_TILE, d), lambda j: (j, 0)),
            pl.BlockSpec((None, None, 1, mods.shape[-1]), mod_idx),
            pl.BlockSpec((1, d), const2),
            pl.BlockSpec((None, d, in_dim), lambda j: (layer, 0, 0)),
            pl.BlockSpec((1, attn_dim), const2),
            pl.BlockSpec((1, KV_DIM), const2),
            pl.BlockSpec((ROW_TILE, LANES), rope_idx),
            pl.BlockSpec((ROW_TILE, LANES), rope_idx),
            pl.BlockSpec(s_q.shape, const2),
            pl.BlockSpec(s_k.shape, const2),
            pl.BlockSpec((1, gmlp_dim), const2),
            pl.BlockSpec(ws.shape, lambda j: (0, 0, 0)),
            pl.BlockSpec(bs.shape, const2),
        ],
        out_specs=[
            pl.BlockSpec((ROW_TILE, attn_dim), lambda j: (j, 0)),
            pl.BlockSpec((ROW_TILE, KV_DIM), lambda j: (j, 0)),
            pl.BlockSpec((ROW_TILE // KEY_TILE, KV_DIM, KEY_TILE), lambda j: (j, 0, 0)),
            pl.BlockSpec((ROW_TILE, gmlp_dim), lambda j: (j, 0)),
        ],
        out_shape=[
            jax.ShapeDtypeStruct((rows, attn_dim), BF16),
            jax.ShapeDtypeStruct((rows, KV_DIM), BF16),
            jax.ShapeDtypeStruct((rows // KEY_TILE, KV_DIM, KEY_TILE), BF16),
            jax.ShapeDtypeStruct((rows, gmlp_dim), BF16),
        ],
        compiler_params=_params("arbitrary"),
        name="inproj",
    )(xs, mods, g1, w_in, qg, kg, cos_t, sin_t, s_q, s_k, gng, ws, bs)


def _attn_kernel(q_ref, kl_ref, kc_ref, vtl_ref, vtc_ref, o_ref, s_ref, *, lat_q_tiles, has_ctx_queries):
    tq = q_ref.shape[0]
    group_w = Q_PER_KV * HEAD_DIM
    n_slabs = vtl_ref.shape[0]
    heads = range(N_KV_HEADS)
    col_chunks = [slice(c * SOFTMAX_COLS, (c + 1) * SOFTMAX_COLS) for c in range(Q_PER_KV * tq // SOFTMAX_COLS)]

    def stacked_qt(kvh):
        qt = q_ref[:, kvh * group_w:(kvh + 1) * group_w].astype(F32).T
        qst = jnp.concatenate([qt[h * HEAD_DIM:(h + 1) * HEAD_DIM, :] for h in range(Q_PER_KV)], axis=1)
        parts = [jnp.zeros_like(qst)] * N_KV_HEADS
        parts[kvh] = qst
        return jnp.concatenate(parts, axis=0).astype(BF16)

    def first_block(s, vt_blk):
        m = jnp.max(s, axis=0, keepdims=True)
        p = jnp.exp2(s - m)
        return m, jnp.sum(p, axis=0, keepdims=True), _dot(vt_blk, p.astype(BF16))

    def next_block(s, vt_blk, carry):
        m, l, acc = carry
        m_new = jnp.maximum(m, jnp.max(s, axis=0, keepdims=True))
        alpha = jnp.exp2(m - m_new)
        p = jnp.exp2(s - m_new)
        l = alpha * l + jnp.sum(p, axis=0, keepdims=True)
        return m_new, l, alpha * acc + _dot(vt_blk, p.astype(BF16))

    def finish(kvh, l, acc):
        ot = acc / l
        ot = jnp.concatenate([ot[:, h * tq:(h + 1) * tq] for h in range(Q_PER_KV)], axis=0)
        o_ref[:, kvh * group_w:(kvh + 1) * group_w] = ot.T.astype(BF16)

    def vrows(kvh):
        return slice(kvh * HEAD_DIM, (kvh + 1) * HEAD_DIM)

    def put_scores(slot, k_blk, qst):
        for kvh in heads:
            s_ref[slot, kvh] = _dot(k_blk, qst[kvh])

    def first_blocks(slot, vt_blk):
        return tuple(tuple(first_block(s_ref[slot, kvh, :, cols], vt_blk[vrows(kvh), :]) for cols in col_chunks)
                     for kvh in heads)

    def next_blocks(slot, vt_blk, carry):
        return tuple(tuple(next_block(s_ref[slot, kvh, :, cols], vt_blk[vrows(kvh), :], carry[kvh][c])
                           for c, cols in enumerate(col_chunks))
                     for kvh in heads)

    def finish_all(carry):
        for kvh in heads:
            l = jnp.concatenate([chunk[1] for chunk in carry[kvh]], axis=1)
            acc = jnp.concatenate([chunk[2] for chunk in carry[kvh]], axis=1)
            finish(kvh, l, acc)

    def latent_queries():
        qst = [stacked_qt(kvh) for kvh in heads]

        def lat_k(j):
            return kl_ref[pl.ds(pl.multiple_of(j * KEY_TILE, KEY_TILE), KEY_TILE), :]

        put_scores(0, kc_ref[...], qst)
        put_scores(1, lat_k(0), qst)
        carry = first_blocks(0, vtc_ref[0])

        def step(j, slot, carry):
            put_scores(1 - slot, lat_k(j + 1), qst)
            return next_blocks(slot, vtl_ref[j], carry)

        def step_pair(jj, carry):
            return step(2 * jj + 1, 0, step(2 * jj, 1, carry))

        carry = lax.fori_loop(0, (n_slabs - 2) // 2, step_pair, carry)
        carry = step(n_slabs - 2, 1, carry)
        finish_all(next_blocks(0, vtl_ref[n_slabs - 1], carry))

    def context_queries():
        put_scores(0, kc_ref[...], [stacked_qt(kvh) for kvh in heads])
        finish_all(first_blocks(0, vtc_ref[0]))

    if has_ctx_queries:
        i = pl.program_id(1)
        pl.when(i < lat_q_tiles)(latent_queries)
        pl.when(i >= lat_q_tiles)(context_queries)
    else:
        latent_queries()


def _attention(q, k, vt, *, batch, seq, ctx_len, has_ctx_queries):
    rows, attn_dim = q.shape
    lat_rows = batch * seq
    lat_q_tiles = seq // Q_TILE
    ctx_q_tiles = ctx_len // Q_TILE
    n_q_tiles = lat_q_tiles + (ctx_q_tiles if has_ctx_queries else 0)
    out_rows = rows if has_ctx_queries else lat_rows

    def q_idx(b, i):
        lat = b * lat_q_tiles + i
        ctx = lat_rows // Q_TILE + b * ctx_q_tiles + (i - lat_q_tiles)
        return (jnp.where(i < lat_q_tiles, lat, ctx), 0)

    kern = functools.partial(_attn_kernel, lat_q_tiles=lat_q_tiles, has_ctx_queries=has_ctx_queries)
    return pl.pallas_call(
        kern,
        grid=(batch, n_q_tiles),
        in_specs=[
            pl.BlockSpec((Q_TILE, attn_dim), q_idx),
            pl.BlockSpec((seq, KV_DIM), lambda b, i: (b, 0)),
            pl.BlockSpec((ctx_len, KV_DIM), lambda b, i: (lat_rows // ctx_len + b, 0)),
            pl.BlockSpec((seq // KEY_TILE, KV_DIM, KEY_TILE), lambda b, i: (b, 0, 0)),
            pl.BlockSpec((ctx_len // KEY_TILE, KV_DIM, KEY_TILE),
                         lambda b, i: (lat_rows // ctx_len + b, 0, 0)),
        ],
        out_specs=pl.BlockSpec((Q_TILE, attn_dim), q_idx),
        out_shape=jax.ShapeDtypeStruct((out_rows, attn_dim), BF16),
        scratch_shapes=[pltpu.VMEM((2, N_KV_HEADS, KEY_TILE, Q_PER_KV * Q_TILE), F32)],
        compiler_params=_params("arbitrary", "arbitrary"),
        name="attention",
    )(q, k, k, vt, vt)


def _out_ffn_kernel(x_ref, at_ref, gm_ref, mod_ref, g2_ref, wo_ref, w1_ref, w2_ref, o_ref, *, d_model, ff_tile):
    mod = mod_ref[...]
    gate1 = mod[:, 2 * d_model:3 * d_model]
    shift = mod[:, 3 * d_model:4 * d_model]
    scale = mod[:, 4 * d_model:5 * d_model]
    gate2 = mod[:, 5 * d_model:6 * d_model]
    attn_dim = at_ref.shape[1]
    y = _dot(at_ref[...], wo_ref[0:attn_dim, :]) + _dot(gm_ref[...], wo_ref[attn_dim:, :])
    x1 = x_ref[...] + gate1 * y
    ms = jnp.mean(x1 * x1, axis=-1, keepdims=True)
    h = (x1 * lax.rsqrt(ms + EPS)) * g2_ref[...]
    hb = (h * (1.0 + scale) + shift).astype(BF16)
    acc = None
    for c in range(w1_ref.shape[1] // ff_tile):
        cols = slice(c * ff_tile, (c + 1) * ff_tile)
        t = jnp.maximum(_dot(hb, w1_ref[:, cols]), 0.0)
        part = _dot((t * t).astype(BF16), w2_ref[cols, :])
        acc = part if acc is None else acc + part
    o_ref[...] = x1 + gate2 * acc


def _out_ffn(layer, xs, attn, gm, mods, g2, w_out, w_ff1, w_ff2, *, n_tiles, lat_tiles, tiles_per_batch):
    d = xs.shape[1]
    ff = w_ff1.shape[-1]
    ctx_mod_row = lat_tiles // tiles_per_batch

    def mod_idx(j):
        return (layer, jnp.where(j < lat_tiles, j // tiles_per_batch, ctx_mod_row), 0, 0)

    resident = pl.Buffered(1)
    kern = functools.partial(_out_ffn_kernel, d_model=d, ff_tile=1024)
    return pl.pallas_call(
        kern,
        grid=(n_tiles,),
        in_specs=[
            pl.BlockSpec((ROW_TILE, d), lambda j: (j, 0)),
            pl.BlockSpec((ROW_TILE, attn.shape[1]), lambda j: (j, 0)),
            pl.BlockSpec((ROW_TILE, gm.shape[1]), lambda j: (j, 0)),
            pl.BlockSpec((None, None, 1, mods.shape[-1]), mod_idx),
            pl.BlockSpec((1, d), lambda j: (0, 0)),
            pl.BlockSpec((None,) + w_out.shape[1:], lambda j: (layer, 0, 0), pipeline_mode=resident),
            pl.BlockSpec((None, d, ff), lambda j: (layer, 0, 0), pipeline_mode=resident),
            pl.BlockSpec((None, ff, d), lambda j: (layer, 0, 0), pipeline_mode=resident),
        ],
        out_specs=pl.BlockSpec((ROW_TILE, d), lambda j: (j, 0)),
        out_shape=jax.ShapeDtypeStruct((n_tiles * ROW_TILE, d), F32),
        compiler_params=_params("arbitrary"),
        name="out_ffn",
    )(xs, attn, gm, mods, g2, w_out, w_ff1, w_ff2)


def _rope_tables(seq):
    t = jnp.arange(seq, dtype=jnp.int32)
    row = (t // GRID_W).astype(F32)
    col = (t % GRID_W).astype(F32)
    n_freq = HEAD_DIM // 4
    inv_freq = ROPE_THETA ** (-jnp.arange(n_freq, dtype=F32) / n_freq)
    lane = np.arange(LANES)
    in_head = lane % HEAD_DIM
    axis = in_head // (HEAD_DIM // 2)
    freq = in_head % n_freq
    sign = np.where((in_head // n_freq) % 2 == 1, 1.0, -1.0).astype(np.float32)
    pos = jnp.where(jnp.asarray(axis)[None, :] == 0, row[:, None], col[:, None])
    ang = pos * inv_freq[jnp.asarray(freq)][None, :]
    cos_t = jnp.concatenate([jnp.cos(ang), jnp.ones((ROW_TILE, LANES), F32)], axis=0)
    sin_t = jnp.concatenate([jnp.sin(ang) * jnp.asarray(sign)[None, :], jnp.zeros((ROW_TILE, LANES), F32)], axis=0)
    return cos_t, sin_t


def _head_indicator(width):
    idx = np.arange(width) // HEAD_DIM
    return jnp.asarray((idx[:, None] == idx[None, :]).astype(np.float32), dtype=BF16)


def kernel(x, c, ctx, c_ctx, w_mod, b_mod, norm1_g, w_in, q_norm_g, k_norm_g, gmlp_norm_g, w_spatial, b_spatial,
           w_out, norm2_g, w_ff1, w_ff2):
    batch, seq, d = x.shape
    ctx_len = ctx.shape[1]
    depth = w_mod.shape[0]
    gmlp_dim = gmlp_norm_g.shape[-1]
    attn_dim = w_out.shape[1] - gmlp_dim
    lat_rows = batch * seq
    lat_tiles = lat_rows // ROW_TILE
    tiles_per_batch = seq // ROW_TILE

    cond = jnp.concatenate([c, c_ctx[None, :], jnp.zeros((MOD_ROWS - batch - 1, d), F32)], axis=0)
    mods = _modulation(cond, w_mod, b_mod).reshape(depth, MOD_ROWS, 1, N_MOD * d)

    cos_t, sin_t = _rope_tables(seq)
    s_q = _head_indicator(attn_dim)
    s_k = _head_indicator(KV_DIM)
    w_in_b, w_out_b = w_in.astype(BF16), w_out.astype(BF16)
    w_ff1_b, w_ff2_b = w_ff1.astype(BF16), w_ff2.astype(BF16)
    w_sp_b = w_spatial.astype(BF16)

    xs = jnp.concatenate([x.reshape(lat_rows, d), ctx.reshape(batch * ctx_len, d)], axis=0)
    for l in range(depth):
        last = l == depth - 1
        q, k, vt, gm = _inproj(
            l, xs, mods, norm1_g[l][None, :], w_in_b,
            jnp.tile(q_norm_g[l], attn_dim // HEAD_DIM)[None, :],
            jnp.tile(k_norm_g[l], KV_DIM // HEAD_DIM)[None, :],
            cos_t, sin_t, s_q, s_k, gmlp_norm_g[l][None, :], w_sp_b[l], b_spatial[l].T,
            lat_tiles=lat_tiles, tiles_per_batch=tiles_per_batch)
        attn = _attention(q, k, vt, batch=batch, seq=seq, ctx_len=ctx_len, has_ctx_queries=not last)
        n_tiles = lat_tiles if last else xs.shape[0] // ROW_TILE
        xs = _out_ffn(l, xs, attn, gm, mods, norm2_g[l][None, :], w_out_b, w_ff1_b, w_ff2_b,
                      n_tiles=n_tiles, lat_tiles=lat_tiles, tiles_per_batch=tiles_per_batch)
    return xs.reshape(batch, seq, d)
```

```python
import functools

import numpy as np
import jax
import jax.numpy as jnp
from jax import lax
from jax.experimental import pallas as pl
from jax.experimental.pallas import tpu as pltpu

F32 = jnp.float32
BF16 = jnp.bfloat16

HEAD_DIM = 64
Q_PER_KV = 4
N_KV_HEADS = 2
KV_DIM = N_KV_HEADS * HEAD_DIM
GROUP_DIM = 128
N_GROUPS = 4
GRID_W = 64
ROPE_THETA = 10000.0
EPS = 1e-6
N_MOD = 6
Q_SCALE = HEAD_DIM ** -0.5
LOG2_E = 1.4426950408889634

LANES = 128
ROW_TILE = 512
KEY_TILE = 256
SOFTMAX_COLS = 512
SCORE_LOOKAHEAD = 1
SCORE_SLOTS = SCORE_LOOKAHEAD + 1
SUM_ROWS = 16
V_ROWS = HEAD_DIM + SUM_ROWS
Q_TILE = 128
MOD_COL_TILE = 1536
MOD_ROWS = 8
VMEM_LIMIT = 56 * 1024 * 1024


def _dot(a, b):
    return jnp.dot(a, b, preferred_element_type=F32)


def _split_bf16(t):
    hi = t.astype(BF16)
    lo = (t - hi.astype(F32)).astype(BF16)
    return hi, lo


def _params(*sem):
    return pltpu.CompilerParams(dimension_semantics=sem, vmem_limit_bytes=VMEM_LIMIT)


def _mod_kernel(cond_ref, w_ref, b_ref, o_ref):
    cnd = cond_ref[...]
    act = cnd * (1.0 / (1.0 + jnp.exp(-cnd)))
    a_hi, a_lo = _split_bf16(act)
    w_hi, w_lo = _split_bf16(w_ref[...])
    acc = _dot(a_hi, w_hi) + _dot(a_lo, w_hi) + _dot(a_hi, w_lo)
    o_ref[...] = acc + b_ref[...]


def _modulation(cond, w_mod, b_mod):
    depth, d, n_out = w_mod.shape
    return pl.pallas_call(
        _mod_kernel,
        grid=(depth, n_out // MOD_COL_TILE),
        in_specs=[
            pl.BlockSpec((MOD_ROWS, d), lambda l, j: (0, 0)),
            pl.BlockSpec((None, d, MOD_COL_TILE), lambda l, j: (l, 0, j)),
            pl.BlockSpec((None, 1, MOD_COL_TILE), lambda l, j: (l, 0, j)),
        ],
        out_specs=pl.BlockSpec((None, MOD_ROWS, MOD_COL_TILE), lambda l, j: (l, 0, j)),
        out_shape=jax.ShapeDtypeStruct((depth, MOD_ROWS, n_out), F32),
        compiler_params=_params("arbitrary", "arbitrary"),
        name="modulation",
    )(cond, w_mod, b_mod.reshape(depth, 1, n_out))


def _inproj_kernel(x_ref, mod_ref, g1_ref, w_ref, qg_ref, kg_ref, cos_ref, sin_ref, s_q_ref, s_k_ref,
                   gng_ref, ws_ref, bs_ref, q_ref, k_ref, vt_ref, gm_ref, *, d_model, attn_dim, gmlp_dim):
    tm = x_ref.shape[0]
    x = x_ref[...]
    mod = mod_ref[...]
    shift, scale = mod[:, 0:d_model], mod[:, d_model:2 * d_model]
    ms = jnp.mean(x * x, axis=-1, keepdims=True)
    h = (x * lax.rsqrt(ms + EPS)) * g1_ref[...]
    hb = (h * (1.0 + scale) + shift).astype(BF16)

    cos, sin = cos_ref[...], sin_ref[...]
    lane = lax.broadcasted_iota(jnp.int32, (tm, LANES), 1)
    second_half = (lane & (HEAD_DIM // 4)) != 0

    def rope(t):
        partner = jnp.where(second_half, pltpu.roll(t, HEAD_DIM // 4, 1),
                            pltpu.roll(t, LANES - HEAD_DIM // 4, 1))
        return t * cos + partner * sin

    def head_mean_sq(t, s_ref):
        hi, lo = _split_bf16(t * t)
        return (_dot(hi, s_ref[...]) + _dot(lo, s_ref[...])) * (1.0 / HEAD_DIM)

    kv_lo = attn_dim
    u_lo = attn_dim + 2 * KV_DIM
    g_lo = u_lo + gmlp_dim

    zq = _dot(hb, w_ref[:, 0:attn_dim])
    qn = (zq * lax.rsqrt(head_mean_sq(zq, s_q_ref) + EPS)) * qg_ref[...]
    for j in range(attn_dim // LANES):
        cols = slice(j * LANES, (j + 1) * LANES)
        q_ref[:, cols] = (rope(qn[:, cols]) * (Q_SCALE * LOG2_E)).astype(BF16)

    zkv = _dot(hb, w_ref[:, kv_lo:u_lo])
    zk, zv = zkv[:, 0:KV_DIM], zkv[:, KV_DIM:2 * KV_DIM]
    kn = (zk * lax.rsqrt(head_mean_sq(zk, s_k_ref) + EPS)) * kg_ref[...]
    k_ref[...] = rope(kn).astype(BF16)
    vt = zv.T
    ones = jnp.ones((SUM_ROWS, KEY_TILE), BF16)
    for c in range(tm // KEY_TILE):
        keys = slice(c * KEY_TILE, (c + 1) * KEY_TILE)
        for kvh in range(N_KV_HEADS):
            vt_ref[c, kvh * V_ROWS:kvh * V_ROWS + HEAD_DIM, :] = vt[kvh * HEAD_DIM:(kvh + 1) * HEAD_DIM, keys].astype(BF16)
            vt_ref[c, kvh * V_ROWS + HEAD_DIM:(kvh + 1) * V_ROWS, :] = ones

    u = jax.nn.gelu(_dot(hb, w_ref[:, u_lo:g_lo]), approximate=True)
    gate = jax.nn.gelu(_dot(hb, w_ref[:, g_lo:g_lo + gmlp_dim]), approximate=True)
    for g in range(N_GROUPS):
        cols = slice(g * GROUP_DIM, (g + 1) * GROUP_DIM)
        t = gate[:, cols]
        dev = t - jnp.mean(t, axis=-1, keepdims=True)
        var = jnp.mean(dev * dev, axis=-1, keepdims=True)
        vn = ((dev * lax.rsqrt(var + EPS)) * gng_ref[:, cols]).astype(BF16)
        for c in range(tm // GROUP_DIM):
            rows = slice(c * GROUP_DIM, (c + 1) * GROUP_DIM)
            mixed = _dot(ws_ref[g], vn[rows, :]) + bs_ref[:, g:g + 1]
            gm_ref[rows, cols] = (u[rows, cols] * mixed).astype(BF16)


def _inproj(layer, xs, mods, g1, w_in, qg, kg, cos_t, sin_t, s_q, s_k, gng, ws, bs, *, lat_tiles, tiles_per_batch):
    rows, d = xs.shape
    in_dim = w_in.shape[-1]
    attn_dim = qg.shape[-1]
    gmlp_dim = gng.shape[-1]
    n_tiles = rows // ROW_TILE
    ctx_mod_row = lat_tiles // tiles_per_batch
    ident_tile = cos_t.shape[0] // ROW_TILE - 1

    def mod_idx(j):
        return (layer, jnp.where(j < lat_tiles, j // tiles_per_batch, ctx_mod_row), 0, 0)

    def rope_idx(j):
        return (jnp.where(j < lat_tiles, j % tiles_per_batch, ident_tile), 0)

    const2 = lambda j: (0, 0)
    kern = functools.partial(_inproj_kernel, d_model=d, attn_dim=attn_dim, gmlp_dim=gmlp_dim)
    return pl.pallas_call(
        kern,
        grid=(n_tiles,),
        in_specs=[
            pl.BlockSpec((ROW_TILE, d), lambda j: (j, 0)),
            pl.BlockSpec((None, None, 1, mods.shape[-1]), mod_idx),
            pl.BlockSpec((1, d), const2),
            pl.BlockSpec((None, d, in_dim), lambda j: (layer, 0, 0)),
            pl.BlockSpec((1, attn_dim), const2),
            pl.BlockSpec((1, KV_DIM), const2),
            pl.BlockSpec((ROW_TILE, LANES), rope_idx),
            pl.BlockSpec((ROW_TILE, LANES), rope_idx),
            pl.BlockSpec(s_q.shape, const2),
            pl.BlockSpec(s_k.shape, const2),
            pl.BlockSpec((1, gmlp_dim), const2),
            pl.BlockSpec(ws.shape, lambda j: (0, 0, 0)),
            pl.BlockSpec(bs.shape, const2),
        ],
        out_specs=[
            pl.BlockSpec((ROW_TILE, attn_dim), lambda j: (j, 0)),
            pl.BlockSpec((ROW_TILE, KV_DIM), lambda j: (j, 0)),
            pl.BlockSpec((ROW_TILE // KEY_TILE, N_KV_HEADS * V_ROWS, KEY_TILE), lambda j: (j, 0, 0)),
            pl.BlockSpec((ROW_TILE, gmlp_dim), lambda j: (j, 0)),
        ],
        out_shape=[
            jax.ShapeDtypeStruct((rows, attn_dim), BF16),
            jax.ShapeDtypeStruct((rows, KV_DIM), BF16),
            jax.ShapeDtypeStruct((rows // KEY_TILE, N_KV_HEADS * V_ROWS, KEY_TILE), BF16),
            jax.ShapeDtypeStruct((rows, gmlp_dim), BF16),
        ],
        compiler_params=_params("arbitrary"),
        name="inproj",
    )(xs, mods, g1, w_in, qg, kg, cos_t, sin_t, s_q, s_k, gng, ws, bs)


def _attn_kernel(q_ref, kl_ref, kc_ref, vtl_ref, vtc_ref, o_ref, s_ref, *, lat_q_tiles, has_ctx_queries):
    tq = q_ref.shape[0]
    group_w = Q_PER_KV * HEAD_DIM
    n_ctx_blocks = vtc_ref.shape[0]
    n_blocks = n_ctx_blocks + vtl_ref.shape[0]
    heads = range(N_KV_HEADS)
    col_chunks = [slice(c * SOFTMAX_COLS, (c + 1) * SOFTMAX_COLS) for c in range(Q_PER_KV * tq // SOFTMAX_COLS)]

    def key_block(j):
        ref, jj = (kc_ref, j) if j < n_ctx_blocks else (kl_ref, j - n_ctx_blocks)
        return ref[jj * KEY_TILE:(jj + 1) * KEY_TILE, :]

    def value_block(j):
        return vtc_ref[j] if j < n_ctx_blocks else vtl_ref[j - n_ctx_blocks]

    def stacked_qt(kvh):
        qt = q_ref[:, kvh * group_w:(kvh + 1) * group_w].astype(F32).T
        qst = jnp.concatenate([qt[h * HEAD_DIM:(h + 1) * HEAD_DIM, :] for h in range(Q_PER_KV)], axis=1)
        parts = [jnp.zeros_like(qst)] * N_KV_HEADS
        parts[kvh] = qst
        return jnp.concatenate(parts, axis=0).astype(BF16)

    def put_scores(slot, k_blk, qst):
        for kvh in heads:
            s_ref[slot, kvh] = _dot(k_blk, qst[kvh])

    def update(s, vt_blk, carry):
        m, acc = carry
        m_new = jnp.maximum(m, jnp.max(s, axis=0, keepdims=True))
        alpha = jnp.exp2(m - m_new)
        p = jnp.exp2(s - m_new)
        return m_new, alpha * acc + _dot(vt_blk, p.astype(BF16))

    def update_all(slot, vt_blk, carry):
        return tuple(tuple(update(s_ref[slot, kvh, :, cols], vt_blk[kvh * V_ROWS:(kvh + 1) * V_ROWS, :],
                                  carry[kvh][c])
                           for c, cols in enumerate(col_chunks))
                     for kvh in heads)

    def init_carry():
        chain = (jnp.full((1, SOFTMAX_COLS), -jnp.inf, F32), jnp.zeros((V_ROWS, SOFTMAX_COLS), F32))
        return tuple(tuple(chain for _ in col_chunks) for _ in heads)

    def finish_all(carry):
        for kvh in heads:
            acc = jnp.concatenate([chain[1] for chain in carry[kvh]], axis=1)
            ot = acc[0:HEAD_DIM, :] / acc[HEAD_DIM:HEAD_DIM + 1, :]
            ot = jnp.concatenate([ot[:, h * tq:(h + 1) * tq] for h in range(Q_PER_KV)], axis=0)
            o_ref[:, kvh * group_w:(kvh + 1) * group_w] = ot.T.astype(BF16)

    def attend(n_key_blocks):
        qst = [stacked_qt(kvh) for kvh in heads]
        for j in range(min(SCORE_LOOKAHEAD, n_key_blocks)):
            put_scores(j, key_block(j), qst)
        carry = init_carry()
        for j in range(n_key_blocks):
            ahead = j + SCORE_LOOKAHEAD
            if ahead < n_key_blocks:
                put_scores(ahead % SCORE_SLOTS, key_block(ahead), qst)
            carry = update_all(j % SCORE_SLOTS, value_block(j), carry)
        finish_all(carry)

    def latent_queries():
        attend(n_blocks)

    def context_queries():
        attend(n_ctx_blocks)

    if has_ctx_queries:
        i = pl.program_id(1)
        pl.when(i < lat_q_tiles)(latent_queries)
        pl.when(i >= lat_q_tiles)(context_queries)
    else:
        latent_queries()


def _attention(q, k, vt, *, batch, seq, ctx_len, has_ctx_queries):
    rows, attn_dim = q.shape
    lat_rows = batch * seq
    lat_q_tiles = seq // Q_TILE
    ctx_q_tiles = ctx_len // Q_TILE
    n_q_tiles = lat_q_tiles + (ctx_q_tiles if has_ctx_queries else 0)
    out_rows = rows if has_ctx_queries else lat_rows
    vt_rows = vt.shape[1]
    assert seq % KEY_TILE == 0 and ctx_len % KEY_TILE == 0, (seq, ctx_len)

    def q_idx(b, i):
        lat = b * lat_q_tiles + i
        ctx = lat_rows // Q_TILE + b * ctx_q_tiles + (i - lat_q_tiles)
        return (jnp.where(i < lat_q_tiles, lat, ctx), 0)

    kern = functools.partial(_attn_kernel, lat_q_tiles=lat_q_tiles, has_ctx_queries=has_ctx_queries)
    return pl.pallas_call(
        kern,
        grid=(batch, n_q_tiles),
        in_specs=[
            pl.BlockSpec((Q_TILE, attn_dim), q_idx),
            pl.BlockSpec((seq, KV_DIM), lambda b, i: (b, 0)),
            pl.BlockSpec((ctx_len, KV_DIM), lambda b, i: (lat_rows // ctx_len + b, 0)),
            pl.BlockSpec((seq // KEY_TILE, vt_rows, KEY_TILE), lambda b, i: (b, 0, 0)),
            pl.BlockSpec((ctx_len // KEY_TILE, vt_rows, KEY_TILE),
                         lambda b, i: (lat_rows // ctx_len + b, 0, 0)),
        ],
        out_specs=pl.BlockSpec((Q_TILE, attn_dim), q_idx),
        out_shape=jax.ShapeDtypeStruct((out_rows, attn_dim), BF16),
        scratch_shapes=[pltpu.VMEM((SCORE_SLOTS, N_KV_HEADS, KEY_TILE, Q_PER_KV * Q_TILE), F32)],
        compiler_params=_params("arbitrary", "arbitrary"),
        name="attention",
    )(q, k, k, vt, vt)


def _out_ffn_kernel(x_ref, at_ref, gm_ref, mod_ref, g2_ref, wo_ref, w1_ref, w2_ref, o_ref, *, d_model, ff_tile):
    mod = mod_ref[...]
    gate1 = mod[:, 2 * d_model:3 * d_model]
    shift = mod[:, 3 * d_model:4 * d_model]
    scale = mod[:, 4 * d_model:5 * d_model]
    gate2 = mod[:, 5 * d_model:6 * d_model]
    attn_dim = at_ref.shape[1]
    y = _dot(at_ref[...], wo_ref[0:attn_dim, :]) + _dot(gm_ref[...], wo_ref[attn_dim:, :])
    x1 = x_ref[...] + gate1 * y
    ms = jnp.mean(x1 * x1, axis=-1, keepdims=True)
    h = (x1 * lax.rsqrt(ms + EPS)) * g2_ref[...]
    hb = (h * (1.0 + scale) + shift).astype(BF16)
    acc = None
    for c in range(w1_ref.shape[1] // ff_tile):
        cols = slice(c * ff_tile, (c + 1) * ff_tile)
        t = jnp.maximum(_dot(hb, w1_ref[:, cols]), 0.0)
        part = _dot((t * t).astype(BF16), w2_ref[cols, :])
        acc = part if acc is None else acc + part
    o_ref[...] = x1 + gate2 * acc


def _out_ffn(layer, xs, attn, gm, mods, g2, w_out, w_ff1, w_ff2, *, n_tiles, lat_tiles, tiles_per_batch):
    d = xs.shape[1]
    ff = w_ff1.shape[-1]
    ctx_mod_row = lat_tiles // tiles_per_batch

    def mod_idx(j):
        return (layer, jnp.where(j < lat_tiles, j // tiles_per_batch, ctx_mod_row), 0, 0)

    resident = pl.Buffered(1)
    kern = functools.partial(_out_ffn_kernel, d_model=d, ff_tile=1024)
    return pl.pallas_call(
        kern,
        grid=(n_tiles,),
        in_specs=[
            pl.BlockSpec((ROW_TILE, d), lambda j: (j, 0)),
            pl.BlockSpec((ROW_TILE, attn.shape[1]), lambda j: (j, 0)),
            pl.BlockSpec((ROW_TILE, gm.shape[1]), lambda j: (j, 0)),
            pl.BlockSpec((None, None, 1, mods.shape[-1]), mod_idx),
            pl.BlockSpec((1, d), lambda j: (0, 0)),
            pl.BlockSpec((None,) + w_out.shape[1:], lambda j: (layer, 0, 0), pipeline_mode=resident),
            pl.BlockSpec((None, d, ff), lambda j: (layer, 0, 0), pipeline_mode=resident),
            pl.BlockSpec((None, ff, d), lambda j: (layer, 0, 0), pipeline_mode=resident),
        ],
        out_specs=pl.BlockSpec((ROW_TILE, d), lambda j: (j, 0)),
        out_shape=jax.ShapeDtypeStruct((n_tiles * ROW_TILE, d), F32),
        compiler_params=_params("arbitrary"),
        name="out_ffn",
    )(xs, attn, gm, mods, g2, w_out, w_ff1, w_ff2)


def _rope_tables(seq):
    t = jnp.arange(seq, dtype=jnp.int32)
    row = (t // GRID_W).astype(F32)
    col = (t % GRID_W).astype(F32)
    n_freq = HEAD_DIM // 4
    inv_freq = ROPE_THETA ** (-jnp.arange(n_freq, dtype=F32) / n_freq)
    lane = np.arange(LANES)
    in_head = lane % HEAD_DIM
    axis = in_head // (HEAD_DIM // 2)
    freq = in_head % n_freq
    sign = np.where((in_head // n_freq) % 2 == 1, 1.0, -1.0).astype(np.float32)
    pos = jnp.where(jnp.asarray(axis)[None, :] == 0, row[:, None], col[:, None])
    ang = pos * inv_freq[jnp.asarray(freq)][None, :]
    cos_t = jnp.concatenate([jnp.cos(ang), jnp.ones((ROW_TILE, LANES), F32)], axis=0)
    sin_t = jnp.concatenate([jnp.sin(ang) * jnp.asarray(sign)[None, :], jnp.zeros((ROW_TILE, LANES), F32)], axis=0)
    return cos_t, sin_t


def _head_indicator(width):
    idx = np.arange(width) // HEAD_DIM
    return jnp.asarray((idx[:, None] == idx[None, :]).astype(np.float32), dtype=BF16)


def kernel(x, c, ctx, c_ctx, w_mod, b_mod, norm1_g, w_in, q_norm_g, k_norm_g, gmlp_norm_g, w_spatial, b_spatial,
           w_out, norm2_g, w_ff1, w_ff2):
    batch, seq, d = x.shape
    ctx_len = ctx.shape[1]
    depth = w_mod.shape[0]
    gmlp_dim = gmlp_norm_g.shape[-1]
    attn_dim = w_out.shape[1] - gmlp_dim
    lat_rows = batch * seq
    lat_tiles = lat_rows // ROW_TILE
    tiles_per_batch = seq // ROW_TILE

    cond = jnp.concatenate([c, c_ctx[None, :], jnp.zeros((MOD_ROWS - batch - 1, d), F32)], axis=0)
    mods = _modulation(cond, w_mod, b_mod).reshape(depth, MOD_ROWS, 1, N_MOD * d)

    cos_t, sin_t = _rope_tables(seq)
    s_q = _head_indicator(attn_dim)
    s_k = _head_indicator(KV_DIM)
    w_in_b, w_out_b = w_in.astype(BF16), w_out.astype(BF16)
    w_ff1_b, w_ff2_b = w_ff1.astype(BF16), w_ff2.astype(BF16)
    w_sp_b = w_spatial.astype(BF16)

    xs = jnp.concatenate([x.reshape(lat_rows, d), ctx.reshape(batch * ctx_len, d)], axis=0)
    for l in range(depth):
        last = l == depth - 1
        q, k, vt, gm = _inproj(
            l, xs, mods, norm1_g[l][None, :], w_in_b,
            jnp.tile(q_norm_g[l], attn_dim // HEAD_DIM)[None, :],
            jnp.tile(k_norm_g[l], KV_DIM // HEAD_DIM)[None, :],
            cos_t, sin_t, s_q, s_k, gmlp_norm_g[l][None, :], w_sp_b[l], b_spatial[l].T,
            lat_tiles=lat_tiles, tiles_per_batch=tiles_per_batch)
        attn = _attention(q, k, vt, batch=batch, seq=seq, ctx_len=ctx_len, has_ctx_queries=not last)
        n_tiles = lat_tiles if last else xs.shape[0] // ROW_TILE
        xs = _out_ffn(l, xs, attn, gm, mods, norm2_g[l][None, :], w_out_b, w_ff1_b, w_ff2_b,
                      n_tiles=n_tiles, lat_tiles=lat_tiles, tiles_per_batch=tiles_per_batch)
    return xs.reshape(batch, seq, d)
```

```python
import functools

import numpy as np
import jax
import jax.numpy as jnp
from jax import lax
from jax.experimental import pallas as pl
from jax.experimental.pallas import tpu as pltpu

F32 = jnp.float32
BF16 = jnp.bfloat16

HEAD_DIM = 64
Q_PER_KV = 4
N_KV_HEADS = 2
KV_DIM = N_KV_HEADS * HEAD_DIM
GROUP_DIM = 128
N_GROUPS = 4
GRID_W = 64
ROPE_THETA = 10000.0
EPS = 1e-6
N_MOD = 6
Q_SCALE = HEAD_DIM ** -0.5
LOG2_E = 1.4426950408889634

LANES = 128
ROW_TILE = 512
KEY_TILE = 256
SOFTMAX_COLS = 512
SCORE_LOOKAHEAD = 1
SCORE_SLOTS = SCORE_LOOKAHEAD + 1
SUM_ROWS = 16
V_ROWS = HEAD_DIM + SUM_ROWS
Q_TILE = 128
MOD_COL_TILE = 1536
MOD_ROWS = 8
VMEM_LIMIT = 56 * 1024 * 1024


def _dot(a, b):
    return jnp.dot(a, b, preferred_element_type=F32)


def _split_bf16(t):
    hi = t.astype(BF16)
    lo = (t - hi.astype(F32)).astype(BF16)
    return hi, lo


def _params(*sem):
    return pltpu.CompilerParams(dimension_semantics=sem, vmem_limit_bytes=VMEM_LIMIT)


def _mod_kernel(cond_ref, w_ref, b_ref, o_ref):
    cnd = cond_ref[...]
    act = cnd * (1.0 / (1.0 + jnp.exp(-cnd)))
    a_hi, a_lo = _split_bf16(act)
    w_hi, w_lo = _split_bf16(w_ref[...])
    acc = _dot(a_hi, w_hi) + _dot(a_lo, w_hi) + _dot(a_hi, w_lo)
    o_ref[...] = acc + b_ref[...]


def _modulation(cond, w_mod, b_mod):
    depth, d, n_out = w_mod.shape
    return pl.pallas_call(
        _mod_kernel,
        grid=(depth, n_out // MOD_COL_TILE),
        in_specs=[
            pl.BlockSpec((MOD_ROWS, d), lambda l, j: (0, 0)),
            pl.BlockSpec((None, d, MOD_COL_TILE), lambda l, j: (l, 0, j)),
            pl.BlockSpec((None, 1, MOD_COL_TILE), lambda l, j: (l, 0, j)),
        ],
        out_specs=pl.BlockSpec((None, MOD_ROWS, MOD_COL_TILE), lambda l, j: (l, 0, j)),
        out_shape=jax.ShapeDtypeStruct((depth, MOD_ROWS, n_out), F32),
        compiler_params=_params("arbitrary", "arbitrary"),
        name="modulation",
    )(cond, w_mod, b_mod.reshape(depth, 1, n_out))


def _inproj_kernel(x_ref, mod_ref, g1_ref, w_ref, qg_ref, kg_ref, cos_ref, sin_ref, s_q_ref, s_k_ref,
                   gng_ref, ws_ref, bs_ref, q_ref, k_ref, vt_ref, gm_ref, *, d_model, attn_dim, gmlp_dim):
    tm = x_ref.shape[0]
    x = x_ref[...]
    mod = mod_ref[...]
    shift, scale = mod[:, 0:d_model], mod[:, d_model:2 * d_model]
    ms = jnp.mean(x * x, axis=-1, keepdims=True)
    h = (x * lax.rsqrt(ms + EPS)) * g1_ref[...]
    hb = (h * (1.0 + scale) + shift).astype(BF16)

    cos, sin = cos_ref[...], sin_ref[...]
    lane = lax.broadcasted_iota(jnp.int32, (tm, LANES), 1)
    second_half = (lane & (HEAD_DIM // 4)) != 0

    def rope(t):
        partner = jnp.where(second_half, pltpu.roll(t, HEAD_DIM // 4, 1),
                            pltpu.roll(t, LANES - HEAD_DIM // 4, 1))
        return t * cos + partner * sin

    def head_mean_sq(t, s_ref):
        return _dot((t * t).astype(BF16), s_ref[...]) * (1.0 / HEAD_DIM)

    kv_lo = attn_dim
    u_lo = attn_dim + 2 * KV_DIM
    g_lo = u_lo + gmlp_dim

    gate = jax.nn.gelu(_dot(hb, w_ref[:, g_lo:g_lo + gmlp_dim]), approximate=True)
    zq = _dot(hb, w_ref[:, 0:attn_dim])
    zkv = _dot(hb, w_ref[:, kv_lo:u_lo])

    vn = []
    for g in range(N_GROUPS):
        cols = slice(g * GROUP_DIM, (g + 1) * GROUP_DIM)
        t = gate[:, cols]
        dev = t - jnp.mean(t, axis=-1, keepdims=True)
        var = jnp.mean(dev * dev, axis=-1, keepdims=True)
        vn.append(((dev * lax.rsqrt(var + EPS)) * gng_ref[:, cols]).astype(BF16))

    qn = (zq * lax.rsqrt(head_mean_sq(zq, s_q_ref) + EPS)) * qg_ref[...]
    zk, zv = zkv[:, 0:KV_DIM], zkv[:, KV_DIM:2 * KV_DIM]
    kn = (zk * lax.rsqrt(head_mean_sq(zk, s_k_ref) + EPS)) * kg_ref[...]
    u = jax.nn.gelu(_dot(hb, w_ref[:, u_lo:g_lo]), approximate=True)

    for j in range(attn_dim // LANES):
        cols = slice(j * LANES, (j + 1) * LANES)
        q_ref[:, cols] = (rope(qn[:, cols]) * (Q_SCALE * LOG2_E)).astype(BF16)
    k_ref[...] = rope(kn).astype(BF16)
    vt = zv.T
    ones = jnp.ones((SUM_ROWS, KEY_TILE), BF16)
    for c in range(tm // KEY_TILE):
        keys = slice(c * KEY_TILE, (c + 1) * KEY_TILE)
        for kvh in range(N_KV_HEADS):
            vt_ref[c, kvh * V_ROWS:kvh * V_ROWS + HEAD_DIM, :] = vt[kvh * HEAD_DIM:(kvh + 1) * HEAD_DIM, keys].astype(BF16)
            vt_ref[c, kvh * V_ROWS + HEAD_DIM:(kvh + 1) * V_ROWS, :] = ones

    for g in range(N_GROUPS):
        cols = slice(g * GROUP_DIM, (g + 1) * GROUP_DIM)
        for c in range(tm // GROUP_DIM):
            rows = slice(c * GROUP_DIM, (c + 1) * GROUP_DIM)
            mixed = _dot(ws_ref[g], vn[g][rows, :]) + bs_ref[:, g:g + 1]
            gm_ref[rows, cols] = (u[rows, cols] * mixed).astype(BF16)


def _inproj(layer, xs, mods, g1, w_in, qg, kg, cos_t, sin_t, s_q, s_k, gng, ws, bs, *, lat_tiles, tiles_per_batch):
    rows, d = xs.shape
    in_dim = w_in.shape[-1]
    attn_dim = qg.shape[-1]
    gmlp_dim = gng.shape[-1]
    n_tiles = rows // ROW_TILE
    ctx_mod_row = lat_tiles // tiles_per_batch
    ident_tile = cos_t.shape[0] // ROW_TILE - 1

    def mod_idx(j):
        return (layer, jnp.where(j < lat_tiles, j // tiles_per_batch, ctx_mod_row), 0, 0)

    def rope_idx(j):
        return (jnp.where(j < lat_tiles, j % tiles_per_batch, ident_tile), 0)

    const2 = lambda j: (0, 0)
    kern = functools.partial(_inproj_kernel, d_model=d, attn_dim=attn_dim, gmlp_dim=gmlp_dim)
    return pl.pallas_call(
        kern,
        grid=(n_tiles,),
        in_specs=[
            pl.BlockSpec((ROW_TILE, d), lambda j: (j, 0)),
            pl.BlockSpec((None, None, 1, mods.shape[-1]), mod_idx),
            pl.BlockSpec((1, d), const2),
            pl.BlockSpec((None, d, in_dim), lambda j: (layer, 0, 0)),
            pl.BlockSpec((1, attn_dim), const2),
            pl.BlockSpec((1, KV_DIM), const2),
            pl.BlockSpec((ROW_TILE, LANES), rope_idx),
            pl.BlockSpec((ROW_TILE, LANES), rope_idx),
            pl.BlockSpec(s_q.shape, const2),
            pl.BlockSpec(s_k.shape, const2),
            pl.BlockSpec((1, gmlp_dim), const2),
            pl.BlockSpec(ws.shape, lambda j: (0, 0, 0)),
            pl.BlockSpec(bs.shape, const2),
        ],
        out_specs=[
            pl.BlockSpec((ROW_TILE, attn_dim), lambda j: (j, 0)),
            pl.BlockSpec((ROW_TILE, KV_DIM), lambda j: (j, 0)),
            pl.BlockSpec((ROW_TILE // KEY_TILE, N_KV_HEADS * V_ROWS, KEY_TILE), lambda j: (j, 0, 0)),
            pl.BlockSpec((ROW_TILE, gmlp_dim), lambda j: (j, 0)),
        ],
        out_shape=[
            jax.ShapeDtypeStruct((rows, attn_dim), BF16),
            jax.ShapeDtypeStruct((rows, KV_DIM), BF16),
            jax.ShapeDtypeStruct((rows // KEY_TILE, N_KV_HEADS * V_ROWS, KEY_TILE), BF16),
            jax.ShapeDtypeStruct((rows, gmlp_dim), BF16),
        ],
        compiler_params=_params("arbitrary"),
        name="inproj",
    )(xs, mods, g1, w_in, qg, kg, cos_t, sin_t, s_q, s_k, gng, ws, bs)


def _attn_kernel(q_ref, kl_ref, kc_ref, vtl_ref, vtc_ref, o_ref, s_ref, *, lat_q_tiles):
    tq = q_ref.shape[0]
    group_w = Q_PER_KV * HEAD_DIM
    n_ctx_blocks = vtc_ref.shape[0]
    n_blocks = n_ctx_blocks + vtl_ref.shape[0]
    heads = range(N_KV_HEADS)
    col_chunks = [slice(c * SOFTMAX_COLS, (c + 1) * SOFTMAX_COLS) for c in range(Q_PER_KV * tq // SOFTMAX_COLS)]

    def key_block(j):
        ref, jj = (kc_ref, j) if j < n_ctx_blocks else (kl_ref, j - n_ctx_blocks)
        return ref[jj * KEY_TILE:(jj + 1) * KEY_TILE, :]

    def value_block(j):
        return vtc_ref[j] if j < n_ctx_blocks else vtl_ref[j - n_ctx_blocks]

    def stacked_qt(kvh):
        qt = q_ref[:, kvh * group_w:(kvh + 1) * group_w].astype(F32).T
        qst = jnp.concatenate([qt[h * HEAD_DIM:(h + 1) * HEAD_DIM, :] for h in range(Q_PER_KV)], axis=1)
        parts = [jnp.zeros_like(qst)] * N_KV_HEADS
        parts[kvh] = qst
        return jnp.concatenate(parts, axis=0).astype(BF16)

    def put_scores(slot, k_blk, qst):
        maxes = []
        for kvh in heads:
            s = _dot(k_blk, qst[kvh])
            s_ref[slot, kvh] = s
            maxes.append(jnp.max(s, axis=0, keepdims=True))
        return maxes

    def update(s, s_max, vt_blk, carry):
        m, acc = carry
        m_new = jnp.maximum(m, s_max)
        alpha = jnp.exp2(m - m_new)
        p = jnp.exp2(s - m_new)
        return m_new, alpha * acc + _dot(vt_blk, p.astype(BF16))

    def update_all(slot, s_max, vt_blk, carry):
        return tuple(tuple(update(s_ref[slot, kvh, :, cols], s_max[kvh][:, cols],
                                  vt_blk[kvh * V_ROWS:(kvh + 1) * V_ROWS, :], carry[kvh][c])
                           for c, cols in enumerate(col_chunks))
                     for kvh in heads)

    def init_carry():
        chain = (jnp.full((1, SOFTMAX_COLS), -jnp.inf, F32), jnp.zeros((V_ROWS, SOFTMAX_COLS), F32))
        return tuple(tuple(chain for _ in col_chunks) for _ in heads)

    def finish_all(carry):
        for kvh in heads:
            acc = jnp.concatenate([chain[1] for chain in carry[kvh]], axis=1)
            ot = acc[0:HEAD_DIM, :] / acc[HEAD_DIM:HEAD_DIM + 1, :]
            ot = jnp.concatenate([ot[:, h * tq:(h + 1) * tq] for h in range(Q_PER_KV)], axis=0)
            o_ref[:, kvh * group_w:(kvh + 1) * group_w] = ot.T.astype(BF16)

    def attend(n_key_blocks):
        qst = [stacked_qt(kvh) for kvh in heads]
        s_max = {}
        for j in range(min(SCORE_LOOKAHEAD, n_key_blocks)):
            s_max[j] = put_scores(j, key_block(j), qst)
        carry = init_carry()
        for j in range(n_key_blocks):
            ahead = j + SCORE_LOOKAHEAD
            if ahead < n_key_blocks:
                s_max[ahead] = put_scores(ahead % SCORE_SLOTS, key_block(ahead), qst)
            carry = update_all(j % SCORE_SLOTS, s_max.pop(j), value_block(j), carry)
        finish_all(carry)

    i = pl.program_id(1)
    pl.when(i < lat_q_tiles)(lambda: attend(n_blocks))
    pl.when(i >= lat_q_tiles)(lambda: attend(n_ctx_blocks))


def _attention(q, k, vt, *, batch, seq, ctx_len):
    rows, attn_dim = q.shape
    lat_rows = batch * seq
    lat_q_tiles = seq // Q_TILE
    ctx_q_tiles = ctx_len // Q_TILE
    n_q_tiles = lat_q_tiles + ctx_q_tiles
    vt_rows = vt.shape[1]
    assert seq % KEY_TILE == 0 and ctx_len % KEY_TILE == 0, (seq, ctx_len)

    def q_idx(b, i):
        lat = b * lat_q_tiles + i
        ctx = lat_rows // Q_TILE + b * ctx_q_tiles + (i - lat_q_tiles)
        return (jnp.where(i < lat_q_tiles, lat, ctx), 0)

    kern = functools.partial(_attn_kernel, lat_q_tiles=lat_q_tiles)
    return pl.pallas_call(
        kern,
        grid=(batch, n_q_tiles),
        in_specs=[
            pl.BlockSpec((Q_TILE, attn_dim), q_idx),
            pl.BlockSpec((seq, KV_DIM), lambda b, i: (b, 0)),
            pl.BlockSpec((ctx_len, KV_DIM), lambda b, i: (lat_rows // ctx_len + b, 0)),
            pl.BlockSpec((seq // KEY_TILE, vt_rows, KEY_TILE), lambda b, i: (b, 0, 0)),
            pl.BlockSpec((ctx_len // KEY_TILE, vt_rows, KEY_TILE),
                         lambda b, i: (lat_rows // ctx_len + b, 0, 0)),
        ],
        out_specs=pl.BlockSpec((Q_TILE, attn_dim), q_idx),
        out_shape=jax.ShapeDtypeStruct((rows, attn_dim), BF16),
        scratch_shapes=[pltpu.VMEM((SCORE_SLOTS, N_KV_HEADS, KEY_TILE, Q_PER_KV * Q_TILE), F32)],
        compiler_params=_params("arbitrary", "arbitrary"),
        name="attention",
    )(q, k, k, vt, vt)


def _out_ffn_kernel(x_ref, at_ref, gm_ref, mod_ref, g2_ref, wo_ref, w1_ref, w2_ref, o_ref, *, d_model, ff_tile):
    mod = mod_ref[...]
    gate1 = mod[:, 2 * d_model:3 * d_model]
    shift = mod[:, 3 * d_model:4 * d_model]
    scale = mod[:, 4 * d_model:5 * d_model]
    gate2 = mod[:, 5 * d_model:6 * d_model]
    attn_dim = at_ref.shape[1]
    y = _dot(at_ref[...], wo_ref[0:attn_dim, :]) + _dot(gm_ref[...], wo_ref[attn_dim:, :])
    x1 = x_ref[...] + gate1 * y
    ms = jnp.mean(x1 * x1, axis=-1, keepdims=True)
    h = (x1 * lax.rsqrt(ms + EPS)) * g2_ref[...]
    hb = (h * (1.0 + scale) + shift).astype(BF16)
    acc = None
    for c in range(w1_ref.shape[1] // ff_tile):
        cols = slice(c * ff_tile, (c + 1) * ff_tile)
        t = jnp.maximum(_dot(hb, w1_ref[:, cols]), 0.0)
        part = _dot((t * t).astype(BF16), w2_ref[cols, :])
        acc = part if acc is None else acc + part
    o_ref[...] = x1 + gate2 * acc


def _out_ffn(layer, xs, attn, gm, mods, g2, w_out, w_ff1, w_ff2, *, n_tiles, lat_tiles, tiles_per_batch):
    d = xs.shape[1]
    ff = w_ff1.shape[-1]
    ctx_mod_row = lat_tiles // tiles_per_batch

    def mod_idx(j):
        return (layer, jnp.where(j < lat_tiles, j // tiles_per_batch, ctx_mod_row), 0, 0)

    resident = pl.Buffered(1)
    kern = functools.partial(_out_ffn_kernel, d_model=d, ff_tile=1024)
    return pl.pallas_call(
        kern,
        grid=(n_tiles,),
        in_specs=[
            pl.BlockSpec((ROW_TILE, d), lambda j: (j, 0)),
            pl.BlockSpec((ROW_TILE, attn.shape[1]), lambda j: (j, 0)),
            pl.BlockSpec((ROW_TILE, gm.shape[1]), lambda j: (j, 0)),
            pl.BlockSpec((None, None, 1, mods.shape[-1]), mod_idx),
            pl.BlockSpec((1, d), lambda j: (0, 0)),
            pl.BlockSpec((None,) + w_out.shape[1:], lambda j: (layer, 0, 0), pipeline_mode=resident),
            pl.BlockSpec((None, d, ff), lambda j: (layer, 0, 0), pipeline_mode=resident),
            pl.BlockSpec((None, ff, d), lambda j: (layer, 0, 0), pipeline_mode=resident),
        ],
        out_specs=pl.BlockSpec((ROW_TILE, d), lambda j: (j, 0)),
        out_shape=jax.ShapeDtypeStruct((n_tiles * ROW_TILE, d), F32),
        compiler_params=_params("arbitrary"),
        name="out_ffn",
    )(xs, attn, gm, mods, g2, w_out, w_ff1, w_ff2)


def _rope_tables(seq):
    t = jnp.arange(seq, dtype=jnp.int32)
    row = (t // GRID_W).astype(F32)
    col = (t % GRID_W).astype(F32)
    n_freq = HEAD_DIM // 4
    inv_freq = ROPE_THETA ** (-jnp.arange(n_freq, dtype=F32) / n_freq)
    lane = np.arange(LANES)
    in_head = lane % HEAD_DIM
    axis = in_head // (HEAD_DIM // 2)
    freq = in_head % n_freq
    sign = np.where((in_head // n_freq) % 2 == 1, 1.0, -1.0).astype(np.float32)
    pos = jnp.where(jnp.asarray(axis)[None, :] == 0, row[:, None], col[:, None])
    ang = pos * inv_freq[jnp.asarray(freq)][None, :]
    cos_t = jnp.concatenate([jnp.cos(ang), jnp.ones((ROW_TILE, LANES), F32)], axis=0)
    sin_t = jnp.concatenate([jnp.sin(ang) * jnp.asarray(sign)[None, :], jnp.zeros((ROW_TILE, LANES), F32)], axis=0)
    return cos_t, sin_t


def _head_indicator(width):
    idx = np.arange(width) // HEAD_DIM
    return jnp.asarray((idx[:, None] == idx[None, :]).astype(np.float32), dtype=BF16)


def kernel(x, c, ctx, c_ctx, w_mod, b_mod, norm1_g, w_in, q_norm_g, k_norm_g, gmlp_norm_g, w_spatial, b_spatial,
           w_out, norm2_g, w_ff1, w_ff2):
    batch, seq, d = x.shape
    ctx_len = ctx.shape[1]
    depth = w_mod.shape[0]
    gmlp_dim = gmlp_norm_g.shape[-1]
    attn_dim = w_out.shape[1] - gmlp_dim
    lat_rows = batch * seq
    lat_tiles = lat_rows // ROW_TILE
    tiles_per_batch = seq // ROW_TILE

    cond = jnp.concatenate([c, c_ctx[None, :], jnp.zeros((MOD_ROWS - batch - 1, d), F32)], axis=0)
    mods = _modulation(cond, w_mod, b_mod).reshape(depth, MOD_ROWS, 1, N_MOD * d)

    cos_t, sin_t = _rope_tables(seq)
    s_q = _head_indicator(attn_dim)
    s_k = _head_indicator(KV_DIM)
    w_in_b, w_out_b = w_in.astype(BF16), w_out.astype(BF16)
    w_ff1_b, w_ff2_b = w_ff1.astype(BF16), w_ff2.astype(BF16)
    w_sp_b = w_spatial.astype(BF16)

    xs = jnp.concatenate([x.reshape(lat_rows, d), ctx.reshape(batch * ctx_len, d)], axis=0)
    for l in range(depth):
        last = l == depth - 1
        q, k, vt, gm = _inproj(
            l, xs, mods, norm1_g[l][None, :], w_in_b,
            jnp.tile(q_norm_g[l], attn_dim // HEAD_DIM)[None, :],
            jnp.tile(k_norm_g[l], KV_DIM // HEAD_DIM)[None, :],
            cos_t, sin_t, s_q, s_k, gmlp_norm_g[l][None, :], w_sp_b[l], b_spatial[l].T,
            lat_tiles=lat_tiles, tiles_per_batch=tiles_per_batch)
        attn = _attention(q, k, vt, batch=batch, seq=seq, ctx_len=ctx_len)
        n_tiles = lat_tiles if last else xs.shape[0] // ROW_TILE
        xs = _out_ffn(l, xs, attn, gm, mods, norm2_g[l][None, :], w_out_b, w_ff1_b, w_ff2_b,
                      n_tiles=n_tiles, lat_tiles=lat_tiles, tiles_per_batch=tiles_per_batch)
    return xs.reshape(batch, seq, d)
```

```python
import functools

import numpy as np
import jax
import jax.numpy as jnp
from jax import lax
from jax.experimental import pallas as pl
from jax.experimental.pallas import tpu as pltpu

F32 = jnp.float32
BF16 = jnp.bfloat16

HEAD_DIM = 64
Q_PER_KV = 4
N_KV_HEADS = 2
KV_DIM = N_KV_HEADS * HEAD_DIM
GROUP_DIM = 128
N_GROUPS = 4
GRID_W = 64
ROPE_THETA = 10000.0
EPS = 1e-6
N_MOD = 6
Q_SCALE = HEAD_DIM ** -0.5
LOG2_E = 1.4426950408889634

LANES = 128
ROW_TILE = 512
KEY_TILE = 256
SOFTMAX_COLS = 512
SCORE_LOOKAHEAD = 1
SCORE_SLOTS = SCORE_LOOKAHEAD + 1
SUM_ROWS = 16
V_ROWS = HEAD_DIM + SUM_ROWS
Q_SUB = 128
Q_TILE = 256
MOD_COL_TILE = 1536
MOD_ROWS = 8
VMEM_LIMIT = 56 * 1024 * 1024


def _dot(a, b):
    return jnp.dot(a, b, preferred_element_type=F32)


def _split_bf16(t):
    hi = t.astype(BF16)
    lo = (t - hi.astype(F32)).astype(BF16)
    return hi, lo


def _params(*sem):
    return pltpu.CompilerParams(dimension_semantics=sem, vmem_limit_bytes=VMEM_LIMIT)


def _mod_kernel(cond_ref, w_ref, b_ref, o_ref):
    cnd = cond_ref[...]
    act = cnd * (1.0 / (1.0 + jnp.exp(-cnd)))
    a_hi, a_lo = _split_bf16(act)
    w_hi, w_lo = _split_bf16(w_ref[...])
    acc = _dot(a_hi, w_hi) + _dot(a_lo, w_hi) + _dot(a_hi, w_lo)
    o_ref[...] = acc + b_ref[...]


def _modulation(cond, w_mod, b_mod):
    depth, d, n_out = w_mod.shape
    return pl.pallas_call(
        _mod_kernel,
        grid=(depth, n_out // MOD_COL_TILE),
        in_specs=[
            pl.BlockSpec((MOD_ROWS, d), lambda l, j: (0, 0)),
            pl.BlockSpec((None, d, MOD_COL_TILE), lambda l, j: (l, 0, j)),
            pl.BlockSpec((None, 1, MOD_COL_TILE), lambda l, j: (l, 0, j)),
        ],
        out_specs=pl.BlockSpec((None, MOD_ROWS, MOD_COL_TILE), lambda l, j: (l, 0, j)),
        out_shape=jax.ShapeDtypeStruct((depth, MOD_ROWS, n_out), F32),
        compiler_params=_params("arbitrary", "arbitrary"),
        name="modulation",
    )(cond, w_mod, b_mod.reshape(depth, 1, n_out))


def _inproj_kernel(*refs, d_model, attn_dim, gmlp_dim, lat_tiles, split_input):
    n_x = 2 if split_input else 1
    x_refs, refs = refs[:n_x], refs[n_x:]
    (mod_ref, g1_ref, w_ref, qg_ref, kg_ref, cos_ref, sin_ref, s_q_ref, s_k_ref, gng_ref, ws_ref, bs_ref,
     wo_f_ref, w1_f_ref, w2_f_ref, q_ref, k_ref, vt_ref, gm_ref, wo_b_ref, w1_b_ref, w2_b_ref) = refs[:22]
    if split_input:
        x = jnp.where(pl.program_id(0) < lat_tiles, x_refs[0][...], x_refs[1][...])
        refs[22][...] = x
    else:
        x = x_refs[0][...]
    tm = x.shape[0]
    for src, dst in ((wo_f_ref, wo_b_ref), (w1_f_ref, w1_b_ref), (w2_f_ref, w2_b_ref)):
        dst[...] = src[...].astype(BF16)
    mod = mod_ref[...]
    shift, scale = mod[:, 0:d_model], mod[:, d_model:2 * d_model]
    ms = jnp.mean(x * x, axis=-1, keepdims=True)
    h = (x * lax.rsqrt(ms + EPS)) * g1_ref[...]
    hb = (h * (1.0 + scale) + shift).astype(BF16)

    cos, sin = cos_ref[...], sin_ref[...]
    lane = lax.broadcasted_iota(jnp.int32, (tm, LANES), 1)
    second_half = (lane & (HEAD_DIM // 4)) != 0

    def rope(t):
        partner = jnp.where(second_half, pltpu.roll(t, HEAD_DIM // 4, 1),
                            pltpu.roll(t, LANES - HEAD_DIM // 4, 1))
        return t * cos + partner * sin

    def head_mean_sq(t, s_ref):
        return _dot((t * t).astype(BF16), s_ref[...]) * (1.0 / HEAD_DIM)

    kv_lo = attn_dim
    u_lo = attn_dim + 2 * KV_DIM
    g_lo = u_lo + gmlp_dim

    gate = jax.nn.gelu(_dot(hb, w_ref[:, g_lo:g_lo + gmlp_dim]), approximate=True)
    zq = _dot(hb, w_ref[:, 0:attn_dim])
    zkv = _dot(hb, w_ref[:, kv_lo:u_lo])

    vn = []
    for g in range(N_GROUPS):
        cols = slice(g * GROUP_DIM, (g + 1) * GROUP_DIM)
        t = gate[:, cols]
        dev = t - jnp.mean(t, axis=-1, keepdims=True)
        var = jnp.mean(dev * dev, axis=-1, keepdims=True)
        vn.append(((dev * lax.rsqrt(var + EPS)) * gng_ref[:, cols]).astype(BF16))

    qn = (zq * lax.rsqrt(head_mean_sq(zq, s_q_ref) + EPS)) * qg_ref[...]
    zk, zv = zkv[:, 0:KV_DIM], zkv[:, KV_DIM:2 * KV_DIM]
    kn = (zk * lax.rsqrt(head_mean_sq(zk, s_k_ref) + EPS)) * kg_ref[...]
    u = jax.nn.gelu(_dot(hb, w_ref[:, u_lo:g_lo]), approximate=True)

    for j in range(attn_dim // LANES):
        cols = slice(j * LANES, (j + 1) * LANES)
        q_ref[:, cols] = (rope(qn[:, cols]) * (Q_SCALE * LOG2_E)).astype(BF16)
    k_ref[...] = rope(kn).astype(BF16)
    vt = zv.T
    ones = jnp.ones((SUM_ROWS, KEY_TILE), BF16)
    for c in range(tm // KEY_TILE):
        keys = slice(c * KEY_TILE, (c + 1) * KEY_TILE)
        for kvh in range(N_KV_HEADS):
            vt_ref[c, kvh * V_ROWS:kvh * V_ROWS + HEAD_DIM, :] = vt[kvh * HEAD_DIM:(kvh + 1) * HEAD_DIM, keys].astype(BF16)
            vt_ref[c, kvh * V_ROWS + HEAD_DIM:(kvh + 1) * V_ROWS, :] = ones

    for g in range(N_GROUPS):
        cols = slice(g * GROUP_DIM, (g + 1) * GROUP_DIM)
        for c in range(tm // GROUP_DIM):
            rows = slice(c * GROUP_DIM, (c + 1) * GROUP_DIM)
            mixed = _dot(ws_ref[g], vn[g][rows, :]) + bs_ref[:, g:g + 1]
            gm_ref[rows, cols] = (u[rows, cols] * mixed).astype(BF16)


def _inproj(layer, x_srcs, mods, g1, w_in, qg, kg, cos_t, sin_t, s_q, s_k, gng, ws, bs, w_out, w_ff1, w_ff2, *,
            lat_tiles, tiles_per_batch):
    split_input = len(x_srcs) == 2
    rows = sum(a.shape[0] for a in x_srcs)
    d = x_srcs[0].shape[1]
    in_dim = w_in.shape[-1]
    attn_dim = qg.shape[-1]
    gmlp_dim = gng.shape[-1]
    n_tiles = rows // ROW_TILE
    ctx_mod_row = lat_tiles // tiles_per_batch
    ident_tile = cos_t.shape[0] // ROW_TILE - 1

    def mod_idx(j):
        return (layer, jnp.where(j < lat_tiles, j // tiles_per_batch, ctx_mod_row), 0, 0)

    def rope_idx(j):
        return (jnp.where(j < lat_tiles, j % tiles_per_batch, ident_tile), 0)

    const2 = lambda j: (0, 0)
    row_spec = pl.BlockSpec((ROW_TILE, d), lambda j: (j, 0))
    if split_input:
        x_specs = [pl.BlockSpec((ROW_TILE, d), lambda j: (jnp.minimum(j, lat_tiles - 1), 0)),
                   pl.BlockSpec((ROW_TILE, d), lambda j: (jnp.maximum(j - lat_tiles, 0), 0))]
    else:
        x_specs = [row_spec]

    def cast_specs(w):
        blk = (w.shape[1] // lat_tiles, w.shape[2])
        return (pl.BlockSpec((None,) + blk, lambda j: (layer, jnp.minimum(j, lat_tiles - 1), 0)),
                pl.BlockSpec(blk, lambda j: (jnp.minimum(j, lat_tiles - 1), 0)),
                jax.ShapeDtypeStruct(w.shape[1:], BF16))

    casts = [cast_specs(w) for w in (w_out, w_ff1, w_ff2)]
    kern = functools.partial(_inproj_kernel, d_model=d, attn_dim=attn_dim, gmlp_dim=gmlp_dim,
                             lat_tiles=lat_tiles, split_input=split_input)
    return pl.pallas_call(
        kern,
        grid=(n_tiles,),
        in_specs=x_specs + [
            pl.BlockSpec((None, None, 1, mods.shape[-1]), mod_idx),
            pl.BlockSpec((1, d), const2),
            pl.BlockSpec((None, d, in_dim), lambda j: (layer, 0, 0)),
            pl.BlockSpec((1, attn_dim), const2),
            pl.BlockSpec((1, KV_DIM), const2),
            pl.BlockSpec((ROW_TILE, LANES), rope_idx),
            pl.BlockSpec((ROW_TILE, LANES), rope_idx),
            pl.BlockSpec(s_q.shape, const2),
            pl.BlockSpec(s_k.shape, const2),
            pl.BlockSpec((1, gmlp_dim), const2),
            pl.BlockSpec(ws.shape, lambda j: (0, 0, 0)),
            pl.BlockSpec(bs.shape, const2),
        ] + [c[0] for c in casts],
        out_specs=[
            pl.BlockSpec((ROW_TILE, attn_dim), lambda j: (j, 0)),
            pl.BlockSpec((ROW_TILE, KV_DIM), lambda j: (j, 0)),
            pl.BlockSpec((ROW_TILE // KEY_TILE, N_KV_HEADS * V_ROWS, KEY_TILE), lambda j: (j, 0, 0)),
            pl.BlockSpec((ROW_TILE, gmlp_dim), lambda j: (j, 0)),
        ] + [c[1] for c in casts] + ([row_spec] if split_input else []),
        out_shape=[
            jax.ShapeDtypeStruct((rows, attn_dim), BF16),
            jax.ShapeDtypeStruct((rows, KV_DIM), BF16),
            jax.ShapeDtypeStruct((rows // KEY_TILE, N_KV_HEADS * V_ROWS, KEY_TILE), BF16),
            jax.ShapeDtypeStruct((rows, gmlp_dim), BF16),
        ] + [c[2] for c in casts] + ([jax.ShapeDtypeStruct((rows, d), F32)] if split_input else []),
        compiler_params=_params("arbitrary"),
        name="inproj",
    )(*x_srcs, mods, g1, w_in, qg, kg, cos_t, sin_t, s_q, s_k, gng, ws, bs, w_out, w_ff1, w_ff2)


def _attn_kernel(q_ref, kl_ref, kc_ref, vtl_ref, vtc_ref, o_ref, s_ref, *, lat_q_tiles):
    tq = Q_SUB
    n_sub = q_ref.shape[0] // Q_SUB
    group_w = Q_PER_KV * HEAD_DIM
    n_ctx_blocks = vtc_ref.shape[0]
    n_blocks = n_ctx_blocks + vtl_ref.shape[0]
    heads = range(N_KV_HEADS)
    col_chunks = [slice(c * SOFTMAX_COLS, (c + 1) * SOFTMAX_COLS) for c in range(Q_PER_KV * tq // SOFTMAX_COLS)]

    def key_block(j):
        ref, jj = (kc_ref, j) if j < n_ctx_blocks else (kl_ref, j - n_ctx_blocks)
        return ref[jj * KEY_TILE:(jj + 1) * KEY_TILE, :]

    def value_block(j):
        return vtc_ref[j] if j < n_ctx_blocks else vtl_ref[j - n_ctx_blocks]

    def stacked_qt(sub, kvh):
        rows = slice(sub * tq, (sub + 1) * tq)
        qt = q_ref[rows, kvh * group_w:(kvh + 1) * group_w].astype(F32).T
        qst = jnp.concatenate([qt[h * HEAD_DIM:(h + 1) * HEAD_DIM, :] for h in range(Q_PER_KV)], axis=1)
        parts = [jnp.zeros_like(qst)] * N_KV_HEADS
        parts[kvh] = qst
        return jnp.concatenate(parts, axis=0).astype(BF16)

    def put_scores(slot, k_blk, qst):
        maxes = []
        for kvh in heads:
            s = _dot(k_blk, qst[kvh])
            s_ref[slot, kvh] = s
            maxes.append(jnp.max(s, axis=0, keepdims=True))
        return maxes

    def update(s, s_max, vt_blk, carry):
        m, acc = carry
        m_new = jnp.maximum(m, s_max)
        alpha = jnp.exp2(m - m_new)
        p = jnp.exp2(s - m_new)
        return m_new, alpha * acc + _dot(vt_blk, p.astype(BF16))

    def update_all(slot, s_max, vt_blk, carry):
        return tuple(tuple(update(s_ref[slot, kvh, :, cols], s_max[kvh][:, cols],
                                  vt_blk[kvh * V_ROWS:(kvh + 1) * V_ROWS, :], carry[kvh][c])
                           for c, cols in enumerate(col_chunks))
                     for kvh in heads)

    def init_carry():
        chain = (jnp.full((1, SOFTMAX_COLS), -jnp.inf, F32), jnp.zeros((V_ROWS, SOFTMAX_COLS), F32))
        return tuple(tuple(chain for _ in col_chunks) for _ in heads)

    def finish_all(sub, carry):
        for kvh in heads:
            acc = jnp.concatenate([chain[1] for chain in carry[kvh]], axis=1)
            ot = acc[0:HEAD_DIM, :] / acc[HEAD_DIM:HEAD_DIM + 1, :]
            ot = jnp.concatenate([ot[:, h * tq:(h + 1) * tq] for h in range(Q_PER_KV)], axis=0)
            o_ref[sub * tq:(sub + 1) * tq, kvh * group_w:(kvh + 1) * group_w] = ot.T.astype(BF16)

    def attend(n_key_blocks):
        items = [(sub, j) for sub in range(n_sub) for j in range(n_key_blocks)]
        qst, s_max = {}, {}

        def put(t):
            sub, j = items[t]
            if sub not in qst:
                qst[sub] = [stacked_qt(sub, kvh) for kvh in heads]
            s_max[t] = put_scores(t % SCORE_SLOTS, key_block(j), qst[sub])

        for t in range(min(SCORE_LOOKAHEAD, len(items))):
            put(t)
        carry = None
        for t, (sub, j) in enumerate(items):
            if t + SCORE_LOOKAHEAD < len(items):
                put(t + SCORE_LOOKAHEAD)
            if j == 0:
                carry = init_carry()
            carry = update_all(t % SCORE_SLOTS, s_max.pop(t), value_block(j), carry)
            if j == n_key_blocks - 1:
                finish_all(sub, carry)

    i = pl.program_id(1)
    pl.when(i < lat_q_tiles)(lambda: attend(n_blocks))
    pl.when(i >= lat_q_tiles)(lambda: attend(n_ctx_blocks))


def _attention(q, k, vt, *, batch, seq, ctx_len):
    rows, attn_dim = q.shape
    lat_rows = batch * seq
    lat_q_tiles = seq // Q_TILE
    ctx_q_tiles = ctx_len // Q_TILE
    n_q_tiles = lat_q_tiles + ctx_q_tiles
    vt_rows = vt.shape[1]
    assert seq % KEY_TILE == 0 and ctx_len % KEY_TILE == 0, (seq, ctx_len)

    def q_idx(b, i):
        lat = b * lat_q_tiles + i
        ctx = lat_rows // Q_TILE + b * ctx_q_tiles + (i - lat_q_tiles)
        return (jnp.where(i < lat_q_tiles, lat, ctx), 0)

    kern = functools.partial(_attn_kernel, lat_q_tiles=lat_q_tiles)
    return pl.pallas_call(
        kern,
        grid=(batch, n_q_tiles),
        in_specs=[
            pl.BlockSpec((Q_TILE, attn_dim), q_idx),
            pl.BlockSpec((seq, KV_DIM), lambda b, i: (b, 0)),
            pl.BlockSpec((ctx_len, KV_DIM), lambda b, i: (lat_rows // ctx_len + b, 0)),
            pl.BlockSpec((seq // KEY_TILE, vt_rows, KEY_TILE), lambda b, i: (b, 0, 0)),
            pl.BlockSpec((ctx_len // KEY_TILE, vt_rows, KEY_TILE),
                         lambda b, i: (lat_rows // ctx_len + b, 0, 0)),
        ],
        out_specs=pl.BlockSpec((Q_TILE, attn_dim), q_idx),
        out_shape=jax.ShapeDtypeStruct((rows, attn_dim), BF16),
        scratch_shapes=[pltpu.VMEM((SCORE_SLOTS, N_KV_HEADS, KEY_TILE, Q_PER_KV * Q_SUB), F32)],
        compiler_params=_params("arbitrary", "arbitrary"),
        name="attention",
    )(q, k, k, vt, vt)


def _out_ffn_kernel(x_ref, at_ref, gm_ref, mod_ref, g2_ref, wo_ref, w1_ref, w2_ref, o_ref, *, d_model, ff_tile):
    mod = mod_ref[...]
    gate1 = mod[:, 2 * d_model:3 * d_model]
    shift = mod[:, 3 * d_model:4 * d_model]
    scale = mod[:, 4 * d_model:5 * d_model]
    gate2 = mod[:, 5 * d_model:6 * d_model]
    attn_dim = at_ref.shape[1]
    y = _dot(at_ref[...], wo_ref[0:attn_dim, :]) + _dot(gm_ref[...], wo_ref[attn_dim:, :])
    x1 = x_ref[...] + gate1 * y
    ms = jnp.mean(x1 * x1, axis=-1, keepdims=True)
    h = (x1 * lax.rsqrt(ms + EPS)) * g2_ref[...]
    hb = (h * (1.0 + scale) + shift).astype(BF16)
    acc = None
    for c in range(w1_ref.shape[1] // ff_tile):
        cols = slice(c * ff_tile, (c + 1) * ff_tile)
        t = jnp.maximum(_dot(hb, w1_ref[:, cols]), 0.0)
        part = _dot((t * t).astype(BF16), w2_ref[cols, :])
        acc = part if acc is None else acc + part
    o_ref[...] = x1 + gate2 * acc


def _out_ffn(layer, xs, attn, gm, mods, g2, w_out, w_ff1, w_ff2, *, n_tiles, lat_tiles, tiles_per_batch):
    d = xs.shape[1]
    ff = w_ff1.shape[-1]
    ctx_mod_row = lat_tiles // tiles_per_batch

    def mod_idx(j):
        return (layer, jnp.where(j < lat_tiles, j // tiles_per_batch, ctx_mod_row), 0, 0)

    resident = pl.Buffered(1)
    kern = functools.partial(_out_ffn_kernel, d_model=d, ff_tile=1024)
    return pl.pallas_call(
        kern,
        grid=(n_tiles,),
        in_specs=[
            pl.BlockSpec((ROW_TILE, d), lambda j: (j, 0)),
            pl.BlockSpec((ROW_TILE, attn.shape[1]), lambda j: (j, 0)),
            pl.BlockSpec((ROW_TILE, gm.shape[1]), lambda j: (j, 0)),
            pl.BlockSpec((None, None, 1, mods.shape[-1]), mod_idx),
            pl.BlockSpec((1, d), lambda j: (0, 0)),
            pl.BlockSpec(w_out.shape, lambda j: (0, 0), pipeline_mode=resident),
            pl.BlockSpec((d, ff), lambda j: (0, 0), pipeline_mode=resident),
            pl.BlockSpec((ff, d), lambda j: (0, 0), pipeline_mode=resident),
        ],
        out_specs=pl.BlockSpec((ROW_TILE, d), lambda j: (j, 0)),
        out_shape=jax.ShapeDtypeStruct((n_tiles * ROW_TILE, d), F32),
        compiler_params=_params("arbitrary"),
        name="out_ffn",
    )(xs, attn, gm, mods, g2, w_out, w_ff1, w_ff2)


def _rope_tables(seq):
    t = jnp.arange(seq, dtype=jnp.int32)
    row = (t // GRID_W).astype(F32)
    col = (t % GRID_W).astype(F32)
    n_freq = HEAD_DIM // 4
    inv_freq = ROPE_THETA ** (-jnp.arange(n_freq, dtype=F32) / n_freq)
    lane = np.arange(LANES)
    in_head = lane % HEAD_DIM
    axis = in_head // (HEAD_DIM // 2)
    freq = in_head % n_freq
    sign = np.where((in_head // n_freq) % 2 == 1, 1.0, -1.0).astype(np.float32)
    pos = jnp.where(jnp.asarray(axis)[None, :] == 0, row[:, None], col[:, None])
    ang = pos * inv_freq[jnp.asarray(freq)][None, :]
    cos_t = jnp.concatenate([jnp.cos(ang), jnp.ones((ROW_TILE, LANES), F32)], axis=0)
    sin_t = jnp.concatenate([jnp.sin(ang) * jnp.asarray(sign)[None, :], jnp.zeros((ROW_TILE, LANES), F32)], axis=0)
    return cos_t, sin_t


def _head_indicator(width):
    idx = np.arange(width) // HEAD_DIM
    return jnp.asarray((idx[:, None] == idx[None, :]).astype(np.float32), dtype=BF16)


def kernel(x, c, ctx, c_ctx, w_mod, b_mod, norm1_g, w_in, q_norm_g, k_norm_g, gmlp_norm_g, w_spatial, b_spatial,
           w_out, norm2_g, w_ff1, w_ff2):
    batch, seq, d = x.shape
    ctx_len = ctx.shape[1]
    depth = w_mod.shape[0]
    gmlp_dim = gmlp_norm_g.shape[-1]
    attn_dim = w_out.shape[1] - gmlp_dim
    lat_rows = batch * seq
    lat_tiles = lat_rows // ROW_TILE
    tiles_per_batch = seq // ROW_TILE

    cond = jnp.concatenate([c, c_ctx[None, :], jnp.zeros((MOD_ROWS - batch - 1, d), F32)], axis=0)
    mods = _modulation(cond, w_mod, b_mod).reshape(depth, MOD_ROWS, 1, N_MOD * d)

    cos_t, sin_t = _rope_tables(seq)
    s_q = _head_indicator(attn_dim)
    s_k = _head_indicator(KV_DIM)
    w_in_b = w_in.astype(BF16)
    w_sp_b = w_spatial.astype(BF16)

    x_srcs = (x.reshape(lat_rows, d), ctx.reshape(batch * ctx_len, d))
    for l in range(depth):
        last = l == depth - 1
        outs = _inproj(
            l, x_srcs, mods, norm1_g[l][None, :], w_in_b,
            jnp.tile(q_norm_g[l], attn_dim // HEAD_DIM)[None, :],
            jnp.tile(k_norm_g[l], KV_DIM // HEAD_DIM)[None, :],
            cos_t, sin_t, s_q, s_k, gmlp_norm_g[l][None, :], w_sp_b[l], b_spatial[l].T, w_out, w_ff1, w_ff2,
            lat_tiles=lat_tiles, tiles_per_batch=tiles_per_batch)
        q, k, vt, gm, w_out_b, w_ff1_b, w_ff2_b = outs[:7]
        xs = outs[7] if len(x_srcs) == 2 else x_srcs[0]
        attn = _attention(q, k, vt, batch=batch, seq=seq, ctx_len=ctx_len)
        n_tiles = lat_tiles if last else xs.shape[0] // ROW_TILE
        xs = _out_ffn(l, xs, attn, gm, mods, norm2_g[l][None, :], w_out_b, w_ff1_b, w_ff2_b,
                      n_tiles=n_tiles, lat_tiles=lat_tiles, tiles_per_batch=tiles_per_batch)
        x_srcs = (xs,)
    return xs.reshape(batch, seq, d)
```

```python
import functools

import numpy as np
import jax
import jax.numpy as jnp
from jax import lax
from jax.experimental import pallas as pl
from jax.experimental.pallas import tpu as pltpu

F32 = jnp.float32
BF16 = jnp.bfloat16

HEAD_DIM = 64
Q_PER_KV = 4
N_KV_HEADS = 2
KV_DIM = N_KV_HEADS * HEAD_DIM
GROUP_DIM = 128
N_GROUPS = 4
GRID_W = 64
ROPE_THETA = 10000.0
EPS = 1e-6
N_MOD = 6
Q_SCALE = HEAD_DIM ** -0.5
LOG2_E = 1.4426950408889634

LANES = 128
ROW_TILE = 512
KEY_TILE = 256
SCORE_LOOKAHEAD = 2
SCORE_SLOTS = SCORE_LOOKAHEAD // 2 + 1
SUM_ROWS = 16
V_ROWS = HEAD_DIM + SUM_ROWS
Q_SUB = 128
Q_TILE = 256
FF_TILE = 1024
MOD_COL_TILE = 2048
MOD_ROWS = 8
VMEM_LIMIT = 56 * 1024 * 1024


def _dot(a, b):
    return jnp.dot(a, b, preferred_element_type=F32)


def _split_bf16(t):
    hi = t.astype(BF16)
    lo = (t - hi.astype(F32)).astype(BF16)
    return hi, lo


def _params(*sem):
    return pltpu.CompilerParams(dimension_semantics=sem, vmem_limit_bytes=VMEM_LIMIT)


def _mod_kernel(cond_ref, w_ref, b_ref, o_ref):
    cnd = cond_ref[...]
    act = cnd * (1.0 / (1.0 + jnp.exp(-cnd)))
    a_hi, a_lo = _split_bf16(act)
    w_hi, w_lo = _split_bf16(w_ref[...])
    acc = _dot(a_hi, w_hi) + _dot(a_lo, w_hi) + _dot(a_hi, w_lo)
    o_ref[...] = acc + b_ref[...]


def _modulation(cond, w_mod, b_mod):
    depth, d, n_out = w_mod.shape
    return pl.pallas_call(
        _mod_kernel,
        grid=(depth, n_out // MOD_COL_TILE),
        in_specs=[
            pl.BlockSpec((MOD_ROWS, d), lambda l, j: (0, 0)),
            pl.BlockSpec((None, d, MOD_COL_TILE), lambda l, j: (l, 0, j)),
            pl.BlockSpec((None, 1, MOD_COL_TILE), lambda l, j: (l, 0, j)),
        ],
        out_specs=pl.BlockSpec((None, MOD_ROWS, MOD_COL_TILE), lambda l, j: (l, 0, j)),
        out_shape=jax.ShapeDtypeStruct((depth, MOD_ROWS, n_out), F32),
        compiler_params=_params("arbitrary", "arbitrary"),
        name="modulation",
    )(cond, w_mod, b_mod.reshape(depth, 1, n_out))


def _inproj_kernel(*refs, d_model, attn_dim, gmlp_dim, lat_tiles, split_input):
    n_x = 2 if split_input else 1
    x_refs, refs = refs[:n_x], refs[n_x:]
    (mod_ref, g1_ref, w_ref, qg_ref, kg_ref, cos_ref, sin_ref, s_q_ref, s_k_ref, gng_ref, ws_ref, bs_ref,
     wo_f_ref, w1_f_ref, w2_f_ref, q_ref, k_ref, vt_ref, gm_ref, wo_b_ref, w1_b_ref, w2_b_ref, *xs_refs) = refs
    if split_input:
        x = jnp.where(pl.program_id(0) < lat_tiles, x_refs[0][...], x_refs[1][...])
        xs_refs[0][...] = x
    else:
        x = x_refs[0][...]
    tm = x.shape[0]
    for src, dst in ((wo_f_ref, wo_b_ref), (w1_f_ref, w1_b_ref), (w2_f_ref, w2_b_ref)):
        dst[...] = src[...].astype(BF16)
    mod = mod_ref[...]
    shift, scale = mod[:, 0:d_model], mod[:, d_model:2 * d_model]
    ms = jnp.mean(x * x, axis=-1, keepdims=True)
    h = (x * lax.rsqrt(ms + EPS)) * g1_ref[...]
    hb = (h * (1.0 + scale) + shift).astype(BF16)

    cos, sin = cos_ref[...], sin_ref[...]
    lane = lax.broadcasted_iota(jnp.int32, (tm, LANES), 1)
    second_half = (lane & (HEAD_DIM // 4)) != 0

    def rope(t):
        partner = jnp.where(second_half, pltpu.roll(t, HEAD_DIM // 4, 1),
                            pltpu.roll(t, LANES - HEAD_DIM // 4, 1))
        return t * cos + partner * sin

    def head_mean_sq(t, s_ref):
        return _dot((t * t).astype(BF16), s_ref[...]) * (1.0 / HEAD_DIM)

    kv_lo = attn_dim
    u_lo = attn_dim + 2 * KV_DIM
    g_lo = u_lo + gmlp_dim

    gate = jax.nn.gelu(_dot(hb, w_ref[:, g_lo:g_lo + gmlp_dim]), approximate=True)
    zq = _dot(hb, w_ref[:, 0:attn_dim])
    zkv = _dot(hb, w_ref[:, kv_lo:u_lo])

    vn = []
    for g in range(N_GROUPS):
        cols = slice(g * GROUP_DIM, (g + 1) * GROUP_DIM)
        t = gate[:, cols]
        dev = t - jnp.mean(t, axis=-1, keepdims=True)
        var = jnp.mean(dev * dev, axis=-1, keepdims=True)
        vn.append(((dev * lax.rsqrt(var + EPS)) * gng_ref[:, cols]).astype(BF16))

    qn = (zq * lax.rsqrt(head_mean_sq(zq, s_q_ref) + EPS)) * qg_ref[...]
    zk, zv = zkv[:, 0:KV_DIM], zkv[:, KV_DIM:2 * KV_DIM]
    kn = (zk * lax.rsqrt(head_mean_sq(zk, s_k_ref) + EPS)) * kg_ref[...]
    u = jax.nn.gelu(_dot(hb, w_ref[:, u_lo:g_lo]), approximate=True)

    for j in range(attn_dim // LANES):
        cols = slice(j * LANES, (j + 1) * LANES)
        q_ref[:, cols] = (rope(qn[:, cols]) * (Q_SCALE * LOG2_E)).astype(BF16)
    k_ref[...] = rope(kn).astype(BF16)
    vt = zv.T
    ones = jnp.ones((SUM_ROWS, KEY_TILE), BF16)
    for c in range(tm // KEY_TILE):
        keys = slice(c * KEY_TILE, (c + 1) * KEY_TILE)
        for kvh in range(N_KV_HEADS):
            vt_ref[c, kvh * V_ROWS:kvh * V_ROWS + HEAD_DIM, :] = vt[kvh * HEAD_DIM:(kvh + 1) * HEAD_DIM, keys].astype(BF16)
            vt_ref[c, kvh * V_ROWS + HEAD_DIM:(kvh + 1) * V_ROWS, :] = ones

    for g in range(N_GROUPS):
        cols = slice(g * GROUP_DIM, (g + 1) * GROUP_DIM)
        for c in range(tm // GROUP_DIM):
            rows = slice(c * GROUP_DIM, (c + 1) * GROUP_DIM)
            mixed = _dot(ws_ref[g], vn[g][rows, :]) + bs_ref[:, g:g + 1]
            gm_ref[rows, cols] = (u[rows, cols] * mixed).astype(BF16)


def _inproj(layer, x_srcs, mods, g1, w_in, qg, kg, cos_t, sin_t, s_q, s_k, gng, ws, bs, w_out, w_ff1, w_ff2, *,
            lat_tiles, tiles_per_batch):
    split_input = len(x_srcs) == 2
    rows = sum(a.shape[0] for a in x_srcs)
    d = x_srcs[0].shape[1]
    in_dim = w_in.shape[-1]
    attn_dim = qg.shape[-1]
    gmlp_dim = gng.shape[-1]
    n_tiles = rows // ROW_TILE
    ctx_mod_row = lat_tiles // tiles_per_batch
    ident_tile = cos_t.shape[0] // ROW_TILE - 1

    def mod_idx(j):
        return (layer, jnp.where(j < lat_tiles, j // tiles_per_batch, ctx_mod_row), 0, 0)

    def rope_idx(j):
        return (jnp.where(j < lat_tiles, j % tiles_per_batch, ident_tile), 0)

    const2 = lambda j: (0, 0)
    row_spec = pl.BlockSpec((ROW_TILE, d), lambda j: (j, 0))
    if split_input:
        x_specs = [pl.BlockSpec((ROW_TILE, d), lambda j: (jnp.minimum(j, lat_tiles - 1), 0)),
                   pl.BlockSpec((ROW_TILE, d), lambda j: (jnp.maximum(j - lat_tiles, 0), 0))]
    else:
        x_specs = [row_spec]

    def cast_specs(w):
        blk = (w.shape[1] // lat_tiles, w.shape[2])
        return (pl.BlockSpec((None,) + blk, lambda j: (layer, jnp.minimum(j, lat_tiles - 1), 0)),
                pl.BlockSpec(blk, lambda j: (jnp.minimum(j, lat_tiles - 1), 0)),
                jax.ShapeDtypeStruct(w.shape[1:], BF16))

    casts = [cast_specs(w) for w in (w_out, w_ff1, w_ff2)]
    kern = functools.partial(_inproj_kernel, d_model=d, attn_dim=attn_dim, gmlp_dim=gmlp_dim,
                             lat_tiles=lat_tiles, split_input=split_input)
    return pl.pallas_call(
        kern,
        grid=(n_tiles,),
        in_specs=x_specs + [
            pl.BlockSpec((None, None, 1, mods.shape[-1]), mod_idx),
            pl.BlockSpec((1, d), const2),
            pl.BlockSpec((None, d, in_dim), lambda j: (layer, 0, 0)),
            pl.BlockSpec((1, attn_dim), const2),
            pl.BlockSpec((1, KV_DIM), const2),
            pl.BlockSpec((ROW_TILE, LANES), rope_idx),
            pl.BlockSpec((ROW_TILE, LANES), rope_idx),
            pl.BlockSpec(s_q.shape, const2),
            pl.BlockSpec(s_k.shape, const2),
            pl.BlockSpec((1, gmlp_dim), const2),
            pl.BlockSpec(ws.shape, lambda j: (0, 0, 0)),
            pl.BlockSpec(bs.shape, const2),
        ] + [c[0] for c in casts],
        out_specs=[
            pl.BlockSpec((ROW_TILE, attn_dim), lambda j: (j, 0)),
            pl.BlockSpec((ROW_TILE, KV_DIM), lambda j: (j, 0)),
            pl.BlockSpec((ROW_TILE // KEY_TILE, N_KV_HEADS * V_ROWS, KEY_TILE), lambda j: (j, 0, 0)),
            pl.BlockSpec((ROW_TILE, gmlp_dim), lambda j: (j, 0)),
        ] + [c[1] for c in casts] + ([row_spec] if split_input else []),
        out_shape=[
            jax.ShapeDtypeStruct((rows, attn_dim), BF16),
            jax.ShapeDtypeStruct((rows, KV_DIM), BF16),
            jax.ShapeDtypeStruct((rows // KEY_TILE, N_KV_HEADS * V_ROWS, KEY_TILE), BF16),
            jax.ShapeDtypeStruct((rows, gmlp_dim), BF16),
        ] + [c[2] for c in casts] + ([jax.ShapeDtypeStruct((rows, d), F32)] if split_input else []),
        compiler_params=_params("arbitrary"),
        name="inproj",
    )(*x_srcs, mods, g1, w_in, qg, kg, cos_t, sin_t, s_q, s_k, gng, ws, bs, w_out, w_ff1, w_ff2)


def _attn_kernel(q_ref, kl_ref, kc_ref, vtl_ref, vtc_ref, o_ref, s_ref, *, lat_q_tiles):
    tq = Q_SUB
    n_sub = q_ref.shape[0] // Q_SUB
    n_cols = Q_PER_KV * tq
    group_w = Q_PER_KV * HEAD_DIM
    heads = range(N_KV_HEADS)
    refs = {"ctx": (kc_ref, vtc_ref), "lat": (kl_ref, vtl_ref)}

    def key_slabs(kinds):
        return [(kind, j) for kind in kinds for j in range(refs[kind][1].shape[0])]

    def stacked_qt(sub, kvh):
        rows = slice(sub * tq, (sub + 1) * tq)
        qt = q_ref[rows, kvh * group_w:(kvh + 1) * group_w].astype(F32).T
        qst = jnp.concatenate([qt[h * HEAD_DIM:(h + 1) * HEAD_DIM, :] for h in range(Q_PER_KV)], axis=1)
        parts = [jnp.zeros_like(qst)] * N_KV_HEADS
        parts[kvh] = qst
        return jnp.concatenate(parts, axis=0).astype(BF16)

    def put_scores(slot, slab, kvh, qst):
        kind, j = slab
        s = _dot(refs[kind][0][j * KEY_TILE:(j + 1) * KEY_TILE, :], qst)
        s_ref[slot, kvh] = s
        return jnp.max(s, axis=0, keepdims=True)

    def update(slot, slab, kvh, s_max, carry):
        kind, j = slab
        m, acc = carry
        m_new = jnp.maximum(m, s_max)
        alpha = jnp.exp2(m - m_new)
        p = jnp.exp2(s_ref[slot, kvh] - m_new)
        vt = refs[kind][1][j, kvh * V_ROWS:(kvh + 1) * V_ROWS, :]
        return m_new, alpha * acc + _dot(vt, p.astype(BF16))

    def finish(sub, kvh, carry):
        acc = carry[1]
        ot = acc[0:HEAD_DIM, :] / acc[HEAD_DIM:HEAD_DIM + 1, :]
        ot = jnp.concatenate([ot[:, h * tq:(h + 1) * tq] for h in range(Q_PER_KV)], axis=0)
        o_ref[sub * tq:(sub + 1) * tq, kvh * group_w:(kvh + 1) * group_w] = ot.T.astype(BF16)

    def attend(slabs):
        items = [(sub, slab, kvh) for sub in range(n_sub) for slab in slabs for kvh in heads]
        qst, s_max, carry = {}, {}, {}

        assert SCORE_LOOKAHEAD % N_KV_HEADS == 0

        def slot(u):
            return (u // N_KV_HEADS) % SCORE_SLOTS

        def put(u):
            sub, slab, kvh = items[u]
            if (sub, kvh) not in qst:
                qst[sub, kvh] = stacked_qt(sub, kvh)
            s_max[u] = put_scores(slot(u), slab, kvh, qst[sub, kvh])

        for u in range(min(SCORE_LOOKAHEAD, len(items))):
            put(u)
        for u, (sub, slab, kvh) in enumerate(items):
            if u + SCORE_LOOKAHEAD < len(items):
                put(u + SCORE_LOOKAHEAD)
            if slab is slabs[0]:
                carry[kvh] = (jnp.full((1, n_cols), -jnp.inf, F32), jnp.zeros((V_ROWS, n_cols), F32))
            carry[kvh] = update(slot(u), slab, kvh, s_max.pop(u), carry[kvh])
            if slab is slabs[-1]:
                finish(sub, kvh, carry[kvh])

    i = pl.program_id(1)
    pl.when(i < lat_q_tiles)(lambda: attend(key_slabs(("ctx", "lat"))))
    pl.when(i >= lat_q_tiles)(lambda: attend(key_slabs(("ctx",))))


def _attention(q, k, vt, *, batch, seq, ctx_len):
    rows, attn_dim = q.shape
    lat_rows = batch * seq
    lat_q_tiles = seq // Q_TILE
    ctx_q_tiles = ctx_len // Q_TILE
    n_q_tiles = lat_q_tiles + ctx_q_tiles
    vt_rows = vt.shape[1]
    assert seq % KEY_TILE == 0 and ctx_len % KEY_TILE == 0, (seq, ctx_len)

    def q_idx(b, i):
        lat = b * lat_q_tiles + i
        ctx = lat_rows // Q_TILE + b * ctx_q_tiles + (i - lat_q_tiles)
        return (jnp.where(i < lat_q_tiles, lat, ctx), 0)

    kern = functools.partial(_attn_kernel, lat_q_tiles=lat_q_tiles)
    return pl.pallas_call(
        kern,
        grid=(batch, n_q_tiles),
        in_specs=[
            pl.BlockSpec((Q_TILE, attn_dim), q_idx),
            pl.BlockSpec((seq, KV_DIM), lambda b, i: (b, 0)),
            pl.BlockSpec((ctx_len, KV_DIM), lambda b, i: (lat_rows // ctx_len + b, 0)),
            pl.BlockSpec((seq // KEY_TILE, vt_rows, KEY_TILE), lambda b, i: (b, 0, 0)),
            pl.BlockSpec((ctx_len // KEY_TILE, vt_rows, KEY_TILE),
                         lambda b, i: (lat_rows // ctx_len + b, 0, 0)),
        ],
        out_specs=pl.BlockSpec((Q_TILE, attn_dim), q_idx),
        out_shape=jax.ShapeDtypeStruct((rows, attn_dim), BF16),
        scratch_shapes=[pltpu.VMEM((SCORE_SLOTS, N_KV_HEADS, KEY_TILE, Q_PER_KV * Q_SUB), F32)],
        compiler_params=_params("arbitrary", "arbitrary"),
        name="attention",
    )(q, k, k, vt, vt)


def _out_ffn_kernel(x_ref, at_ref, gm_ref, mod_ref, g2_ref, wo_ref, w1_ref, w2_ref, o_ref, *, d_model):
    mod = mod_ref[...]
    gate1 = mod[:, 2 * d_model:3 * d_model]
    shift = mod[:, 3 * d_model:4 * d_model]
    scale = mod[:, 4 * d_model:5 * d_model]
    gate2 = mod[:, 5 * d_model:6 * d_model]
    attn_dim = at_ref.shape[1]
    y = _dot(at_ref[...], wo_ref[0:attn_dim, :]) + _dot(gm_ref[...], wo_ref[attn_dim:, :])
    x1 = x_ref[...] + gate1 * y
    ms = jnp.mean(x1 * x1, axis=-1, keepdims=True)
    h = (x1 * lax.rsqrt(ms + EPS)) * g2_ref[...]
    hb = (h * (1.0 + scale) + shift).astype(BF16)
    acc = None
    for c in range(w1_ref.shape[1] // FF_TILE):
        cols = slice(c * FF_TILE, (c + 1) * FF_TILE)
        t = jnp.maximum(_dot(hb, w1_ref[:, cols]), 0.0)
        part = _dot((t * t).astype(BF16), w2_ref[cols, :])
        acc = part if acc is None else acc + part
    o_ref[...] = x1 + gate2 * acc


def _out_ffn(layer, xs, attn, gm, mods, g2, w_out, w_ff1, w_ff2, *, n_tiles, lat_tiles, tiles_per_batch):
    d = xs.shape[1]
    ff = w_ff1.shape[-1]
    ctx_mod_row = lat_tiles // tiles_per_batch

    def mod_idx(j):
        return (layer, jnp.where(j < lat_tiles, j // tiles_per_batch, ctx_mod_row), 0, 0)

    resident = pl.Buffered(1)
    kern = functools.partial(_out_ffn_kernel, d_model=d)
    return pl.pallas_call(
        kern,
        grid=(n_tiles,),
        in_specs=[
            pl.BlockSpec((ROW_TILE, d), lambda j: (j, 0)),
            pl.BlockSpec((ROW_TILE, attn.shape[1]), lambda j: (j, 0)),
            pl.BlockSpec((ROW_TILE, gm.shape[1]), lambda j: (j, 0)),
            pl.BlockSpec((None, None, 1, mods.shape[-1]), mod_idx),
            pl.BlockSpec((1, d), lambda j: (0, 0)),
            pl.BlockSpec(w_out.shape, lambda j: (0, 0), pipeline_mode=resident),
            pl.BlockSpec((d, ff), lambda j: (0, 0), pipeline_mode=resident),
            pl.BlockSpec((ff, d), lambda j: (0, 0), pipeline_mode=resident),
        ],
        out_specs=pl.BlockSpec((ROW_TILE, d), lambda j: (j, 0)),
        out_shape=jax.ShapeDtypeStruct((n_tiles * ROW_TILE, d), F32),
        compiler_params=_params("arbitrary"),
        name="out_ffn",
    )(xs, attn, gm, mods, g2, w_out, w_ff1, w_ff2)


def _rope_tables(seq):
    n_rows = seq // GRID_W
    n_freq = HEAD_DIM // 4
    inv_freq = ROPE_THETA ** (-jnp.arange(n_freq, dtype=F32) / n_freq)
    ang_row = jnp.arange(n_rows, dtype=F32)[:, None] * inv_freq
    ang_col = jnp.arange(GRID_W, dtype=F32)[:, None] * inv_freq

    def table(fn, sign):
        by_row = jnp.repeat(fn(ang_row), GRID_W, axis=0)
        by_col = jnp.tile(fn(ang_col), (n_rows, 1))
        head = jnp.concatenate([sign[0] * by_row, sign[1] * by_row, sign[0] * by_col, sign[1] * by_col], axis=1)
        return jnp.tile(head, (1, LANES // HEAD_DIM))

    cos_t = jnp.concatenate([table(jnp.cos, (1.0, 1.0)), jnp.ones((ROW_TILE, LANES), F32)], axis=0)
    sin_t = jnp.concatenate([table(jnp.sin, (-1.0, 1.0)), jnp.zeros((ROW_TILE, LANES), F32)], axis=0)
    return cos_t, sin_t


def _head_indicator(width):
    idx = np.arange(width) // HEAD_DIM
    return jnp.asarray((idx[:, None] == idx[None, :]).astype(np.float32), dtype=BF16)


def kernel(x, c, ctx, c_ctx, w_mod, b_mod, norm1_g, w_in, q_norm_g, k_norm_g, gmlp_norm_g, w_spatial, b_spatial,
           w_out, norm2_g, w_ff1, w_ff2):
    batch, seq, d = x.shape
    ctx_len = ctx.shape[1]
    depth = w_mod.shape[0]
    gmlp_dim = gmlp_norm_g.shape[-1]
    attn_dim = w_out.shape[1] - gmlp_dim
    lat_rows = batch * seq
    lat_tiles = lat_rows // ROW_TILE
    tiles_per_batch = seq // ROW_TILE

    cond = jnp.concatenate([c, c_ctx[None, :], jnp.zeros((MOD_ROWS - batch - 1, d), F32)], axis=0)
    mods = _modulation(cond, w_mod, b_mod).reshape(depth, MOD_ROWS, 1, N_MOD * d)

    cos_t, sin_t = _rope_tables(seq)
    s_q = _head_indicator(attn_dim)
    s_k = _head_indicator(KV_DIM)
    w_in_b = w_in.astype(BF16)
    w_sp_b = w_spatial.astype(BF16)

    x_srcs = (x.reshape(lat_rows, d), ctx.reshape(batch * ctx_len, d))
    for l in range(depth):
        last = l == depth - 1
        outs = _inproj(
            l, x_srcs, mods, norm1_g[l][None, :], w_in_b,
            jnp.tile(q_norm_g[l], attn_dim // HEAD_DIM)[None, :],
            jnp.tile(k_norm_g[l], KV_DIM // HEAD_DIM)[None, :],
            cos_t, sin_t, s_q, s_k, gmlp_norm_g[l][None, :], w_sp_b[l], b_spatial[l].T, w_out, w_ff1, w_ff2,
            lat_tiles=lat_tiles, tiles_per_batch=tiles_per_batch)
        q, k, vt, gm, w_out_b, w_ff1_b, w_ff2_b = outs[:7]
        xs = outs[7] if len(x_srcs) == 2 else x_srcs[0]
        attn = _attention(q, k, vt, batch=batch, seq=seq, ctx_len=ctx_len)
        n_tiles = lat_tiles if last else xs.shape[0] // ROW_TILE
        xs = _out_ffn(l, xs, attn, gm, mods, norm2_g[l][None, :], w_out_b, w_ff1_b, w_ff2_b,
                      n_tiles=n_tiles, lat_tiles=lat_tiles, tiles_per_batch=tiles_per_batch)
        x_srcs = (xs,)
    return xs.reshape(batch, seq, d)
```

```python
import functools

import numpy as np
import jax
import jax.numpy as jnp
from jax import lax
from jax.experimental import pallas as pl
from jax.experimental.pallas import tpu as pltpu

F32 = jnp.float32
BF16 = jnp.bfloat16

HEAD_DIM = 64
Q_PER_KV = 4
N_KV_HEADS = 2
KV_DIM = N_KV_HEADS * HEAD_DIM
GROUP_DIM = 128
N_GROUPS = 4
GRID_W = 64
ROPE_THETA = 10000.0
EPS = 1e-6
N_MOD = 6
Q_SCALE = HEAD_DIM ** -0.5
LOG2_E = 1.4426950408889634

LANES = 128
ROW_TILE = 512
KEY_TILE = 256
SCORE_LOOKAHEAD = 2
SCORE_SLOTS = SCORE_LOOKAHEAD // 2 + 1
SUM_ROWS = 16
V_ROWS = HEAD_DIM + SUM_ROWS
Q_SUB = 128
Q_TILE = 512
FF_TILE = 1024
MOD_COL_TILE = 2048
MOD_ROWS = 8
VMEM_LIMIT = 56 * 1024 * 1024


def _dot(a, b):
    return jnp.dot(a, b, preferred_element_type=F32)


def _split_bf16(t):
    hi = t.astype(BF16)
    lo = (t - hi.astype(F32)).astype(BF16)
    return hi, lo


def _params(*sem):
    return pltpu.CompilerParams(dimension_semantics=sem, vmem_limit_bytes=VMEM_LIMIT)


def _mod_kernel(cond_ref, w_ref, b_ref, o_ref):
    cnd = cond_ref[...]
    act = cnd * (1.0 / (1.0 + jnp.exp(-cnd)))
    a_hi, a_lo = _split_bf16(act)
    w_hi, w_lo = _split_bf16(w_ref[...])
    acc = _dot(a_hi, w_hi) + _dot(a_lo, w_hi) + _dot(a_hi, w_lo)
    o_ref[...] = acc + b_ref[...]


def _modulation(cond, w_mod, b_mod):
    depth, d, n_out = w_mod.shape
    return pl.pallas_call(
        _mod_kernel,
        grid=(depth, n_out // MOD_COL_TILE),
        in_specs=[
            pl.BlockSpec((MOD_ROWS, d), lambda l, j: (0, 0)),
            pl.BlockSpec((None, d, MOD_COL_TILE), lambda l, j: (l, 0, j)),
            pl.BlockSpec((None, 1, MOD_COL_TILE), lambda l, j: (l, 0, j)),
        ],
        out_specs=pl.BlockSpec((None, MOD_ROWS, MOD_COL_TILE), lambda l, j: (l, 0, j)),
        out_shape=jax.ShapeDtypeStruct((depth, MOD_ROWS, n_out), F32),
        compiler_params=_params("arbitrary", "arbitrary"),
        name="modulation",
    )(cond, w_mod, b_mod.reshape(depth, 1, n_out))


def _inproj_kernel(*refs, d_model, attn_dim, gmlp_dim, lat_tiles, split_input):
    n_x = 2 if split_input else 1
    x_refs, refs = refs[:n_x], refs[n_x:]
    (mod_ref, g1_ref, w_ref, qg_ref, kg_ref, cos_ref, sin_ref, s_q_ref, s_k_ref, gng_ref, ws_ref, bs_ref,
     wo_f_ref, w1_f_ref, w2_f_ref, q_ref, k_ref, vt_ref, gm_ref, wo_b_ref, w1_b_ref, w2_b_ref, *xs_refs) = refs
    if split_input:
        x = jnp.where(pl.program_id(0) < lat_tiles, x_refs[0][...], x_refs[1][...])
        xs_refs[0][...] = x
    else:
        x = x_refs[0][...]
    tm = x.shape[0]
    for src, dst in ((wo_f_ref, wo_b_ref), (w1_f_ref, w1_b_ref), (w2_f_ref, w2_b_ref)):
        dst[...] = src[...].astype(BF16)
    mod = mod_ref[...]
    shift, scale = mod[:, 0:d_model], mod[:, d_model:2 * d_model]
    ms = jnp.mean(x * x, axis=-1, keepdims=True)
    h = (x * lax.rsqrt(ms + EPS)) * g1_ref[...]
    hb = (h * (1.0 + scale) + shift).astype(BF16)

    cos, sin = cos_ref[...], sin_ref[...]
    lane = lax.broadcasted_iota(jnp.int32, (tm, LANES), 1)
    second_half = (lane & (HEAD_DIM // 4)) != 0

    def rope(t):
        partner = jnp.where(second_half, pltpu.roll(t, HEAD_DIM // 4, 1),
                            pltpu.roll(t, LANES - HEAD_DIM // 4, 1))
        return t * cos + partner * sin

    def head_mean_sq(t, s_ref):
        return _dot((t * t).astype(BF16), s_ref[...]) * (1.0 / HEAD_DIM)

    kv_lo = attn_dim
    u_lo = attn_dim + 2 * KV_DIM
    g_lo = u_lo + gmlp_dim

    gate = jax.nn.gelu(_dot(hb, w_ref[:, g_lo:g_lo + gmlp_dim]), approximate=True)
    zq = _dot(hb, w_ref[:, 0:attn_dim])
    zkv = _dot(hb, w_ref[:, kv_lo:u_lo])

    vn = []
    for g in range(N_GROUPS):
        cols = slice(g * GROUP_DIM, (g + 1) * GROUP_DIM)
        t = gate[:, cols]
        dev = t - jnp.mean(t, axis=-1, keepdims=True)
        var = jnp.mean(dev * dev, axis=-1, keepdims=True)
        vn.append(((dev * lax.rsqrt(var + EPS)) * gng_ref[:, cols]).astype(BF16))

    qn = (zq * lax.rsqrt(head_mean_sq(zq, s_q_ref) + EPS)) * qg_ref[...]
    zk, zv = zkv[:, 0:KV_DIM], zkv[:, KV_DIM:2 * KV_DIM]
    kn = (zk * lax.rsqrt(head_mean_sq(zk, s_k_ref) + EPS)) * kg_ref[...]
    u = jax.nn.gelu(_dot(hb, w_ref[:, u_lo:g_lo]), approximate=True)

    for j in range(attn_dim // LANES):
        cols = slice(j * LANES, (j + 1) * LANES)
        q_ref[:, cols] = (rope(qn[:, cols]) * (Q_SCALE * LOG2_E)).astype(BF16)
    k_ref[...] = rope(kn).astype(BF16)
    vt = zv.T
    ones = jnp.ones((SUM_ROWS, KEY_TILE), BF16)
    for c in range(tm // KEY_TILE):
        keys = slice(c * KEY_TILE, (c + 1) * KEY_TILE)
        for kvh in range(N_KV_HEADS):
            vt_ref[c, kvh * V_ROWS:kvh * V_ROWS + HEAD_DIM, :] = vt[kvh * HEAD_DIM:(kvh + 1) * HEAD_DIM, keys].astype(BF16)
            vt_ref[c, kvh * V_ROWS + HEAD_DIM:(kvh + 1) * V_ROWS, :] = ones

    for g in range(N_GROUPS):
        cols = slice(g * GROUP_DIM, (g + 1) * GROUP_DIM)
        for c in range(tm // GROUP_DIM):
            rows = slice(c * GROUP_DIM, (c + 1) * GROUP_DIM)
            mixed = _dot(ws_ref[g], vn[g][rows, :]) + bs_ref[:, g:g + 1]
            gm_ref[rows, cols] = (u[rows, cols] * mixed).astype(BF16)


def _inproj(layer, x_srcs, mods, g1, w_in, qg, kg, cos_t, sin_t, s_q, s_k, gng, ws, bs, w_out, w_ff1, w_ff2, *,
            lat_tiles, tiles_per_batch):
    split_input = len(x_srcs) == 2
    rows = sum(a.shape[0] for a in x_srcs)
    d = x_srcs[0].shape[1]
    in_dim = w_in.shape[-1]
    attn_dim = qg.shape[-1]
    gmlp_dim = gng.shape[-1]
    n_tiles = rows // ROW_TILE
    ctx_mod_row = lat_tiles // tiles_per_batch
    ident_tile = cos_t.shape[0] // ROW_TILE - 1

    def mod_idx(j):
        return (layer, jnp.where(j < lat_tiles, j // tiles_per_batch, ctx_mod_row), 0, 0)

    def rope_idx(j):
        return (jnp.where(j < lat_tiles, j % tiles_per_batch, ident_tile), 0)

    const2 = lambda j: (0, 0)
    row_spec = pl.BlockSpec((ROW_TILE, d), lambda j: (j, 0))
    if split_input:
        x_specs = [pl.BlockSpec((ROW_TILE, d), lambda j: (jnp.minimum(j, lat_tiles - 1), 0)),
                   pl.BlockSpec((ROW_TILE, d), lambda j: (jnp.maximum(j - lat_tiles, 0), 0))]
    else:
        x_specs = [row_spec]

    def cast_specs(w):
        blk = (w.shape[1] // lat_tiles, w.shape[2])
        return (pl.BlockSpec((None,) + blk, lambda j: (layer, jnp.minimum(j, lat_tiles - 1), 0)),
                pl.BlockSpec(blk, lambda j: (jnp.minimum(j, lat_tiles - 1), 0)),
                jax.ShapeDtypeStruct(w.shape[1:], BF16))

    casts = [cast_specs(w) for w in (w_out, w_ff1, w_ff2)]
    kern = functools.partial(_inproj_kernel, d_model=d, attn_dim=attn_dim, gmlp_dim=gmlp_dim,
                             lat_tiles=lat_tiles, split_input=split_input)
    return pl.pallas_call(
        kern,
        grid=(n_tiles,),
        in_specs=x_specs + [
            pl.BlockSpec((None, None, 1, mods.shape[-1]), mod_idx),
            pl.BlockSpec((1, d), const2),
            pl.BlockSpec((None, d, in_dim), lambda j: (layer, 0, 0)),
            pl.BlockSpec((1, attn_dim), const2),
            pl.BlockSpec((1, KV_DIM), const2),
            pl.BlockSpec((ROW_TILE, LANES), rope_idx),
            pl.BlockSpec((ROW_TILE, LANES), rope_idx),
            pl.BlockSpec(s_q.shape, const2),
            pl.BlockSpec(s_k.shape, const2),
            pl.BlockSpec((1, gmlp_dim), const2),
            pl.BlockSpec(ws.shape, lambda j: (0, 0, 0)),
            pl.BlockSpec(bs.shape, const2),
        ] + [c[0] for c in casts],
        out_specs=[
            pl.BlockSpec((ROW_TILE, attn_dim), lambda j: (j, 0)),
            pl.BlockSpec((ROW_TILE, KV_DIM), lambda j: (j, 0)),
            pl.BlockSpec((ROW_TILE // KEY_TILE, N_KV_HEADS * V_ROWS, KEY_TILE), lambda j: (j, 0, 0)),
            pl.BlockSpec((ROW_TILE, gmlp_dim), lambda j: (j, 0)),
        ] + [c[1] for c in casts] + ([row_spec] if split_input else []),
        out_shape=[
            jax.ShapeDtypeStruct((rows, attn_dim), BF16),
            jax.ShapeDtypeStruct((rows, KV_DIM), BF16),
            jax.ShapeDtypeStruct((rows // KEY_TILE, N_KV_HEADS * V_ROWS, KEY_TILE), BF16),
            jax.ShapeDtypeStruct((rows, gmlp_dim), BF16),
        ] + [c[2] for c in casts] + ([jax.ShapeDtypeStruct((rows, d), F32)] if split_input else []),
        compiler_params=_params("arbitrary"),
        name="inproj",
    )(*x_srcs, mods, g1, w_in, qg, kg, cos_t, sin_t, s_q, s_k, gng, ws, bs, w_out, w_ff1, w_ff2)


def _attn_kernel(ql_ref, qc_ref, kl_ref, kc_ref, vtl_ref, vtc_ref, ol_ref, oc_ref, s_ref, *, n_lat_steps):
    tq = Q_SUB
    n_cols = Q_PER_KV * tq
    group_w = Q_PER_KV * HEAD_DIM
    heads = range(N_KV_HEADS)
    refs = {"ctx": (kc_ref, vtc_ref), "lat": (kl_ref, vtl_ref)}

    def key_slabs(kinds):
        return [(kind, j) for kind in kinds for j in range(refs[kind][1].shape[0])]

    def stacked_qt(q_ref, sub, kvh):
        rows = slice(sub * tq, (sub + 1) * tq)
        qt = q_ref[rows, kvh * group_w:(kvh + 1) * group_w].astype(F32).T
        qst = jnp.concatenate([qt[h * HEAD_DIM:(h + 1) * HEAD_DIM, :] for h in range(Q_PER_KV)], axis=1)
        parts = [jnp.zeros_like(qst)] * N_KV_HEADS
        parts[kvh] = qst
        return jnp.concatenate(parts, axis=0).astype(BF16)

    def put_scores(slot, slab, kvh, qst):
        kind, j = slab
        s = _dot(refs[kind][0][j * KEY_TILE:(j + 1) * KEY_TILE, :], qst)
        s_ref[slot, kvh] = s
        return jnp.max(s, axis=0, keepdims=True)

    def update(slot, slab, kvh, s_max, carry):
        kind, j = slab
        m, acc = carry
        m_new = jnp.maximum(m, s_max)
        alpha = jnp.exp2(m - m_new)
        p = jnp.exp2(s_ref[slot, kvh] - m_new)
        vt = refs[kind][1][j, kvh * V_ROWS:(kvh + 1) * V_ROWS, :]
        return m_new, alpha * acc + _dot(vt, p.astype(BF16))

    def finish(o_ref, sub, kvh, carry):
        acc = carry[1]
        ot = acc[0:HEAD_DIM, :] / acc[HEAD_DIM:HEAD_DIM + 1, :]
        ot = jnp.concatenate([ot[:, h * tq:(h + 1) * tq] for h in range(Q_PER_KV)], axis=0)
        o_ref[sub * tq:(sub + 1) * tq, kvh * group_w:(kvh + 1) * group_w] = ot.T.astype(BF16)

    def attend(q_ref, o_ref, slabs):
        n_sub = q_ref.shape[0] // tq
        items = [(sub, slab, kvh) for sub in range(n_sub) for slab in slabs for kvh in heads]
        qst, s_max, carry = {}, {}, {}

        assert SCORE_LOOKAHEAD % N_KV_HEADS == 0

        def slot(u):
            return (u // N_KV_HEADS) % SCORE_SLOTS

        def put(u):
            sub, slab, kvh = items[u]
            if (sub, kvh) not in qst:
                qst[sub, kvh] = stacked_qt(q_ref, sub, kvh)
            s_max[u] = put_scores(slot(u), slab, kvh, qst[sub, kvh])

        for u in range(min(SCORE_LOOKAHEAD, len(items))):
            put(u)
        for u, (sub, slab, kvh) in enumerate(items):
            if u + SCORE_LOOKAHEAD < len(items):
                put(u + SCORE_LOOKAHEAD)
            if slab is slabs[0]:
                carry[kvh] = (jnp.full((1, n_cols), -jnp.inf, F32), jnp.zeros((V_ROWS, n_cols), F32))
            carry[kvh] = update(slot(u), slab, kvh, s_max.pop(u), carry[kvh])
            if slab is slabs[-1]:
                finish(o_ref, sub, kvh, carry[kvh])

    i = pl.program_id(1)
    pl.when(i < n_lat_steps)(lambda: attend(ql_ref, ol_ref, key_slabs(("ctx", "lat"))))
    pl.when(i >= n_lat_steps)(lambda: attend(qc_ref, oc_ref, key_slabs(("ctx",))))


def _attention(q, k, vt, *, batch, seq, ctx_len):
    rows, attn_dim = q.shape
    lat_rows = batch * seq
    n_lat_steps = seq // Q_TILE
    vt_rows = vt.shape[1]
    assert seq % KEY_TILE == 0 and ctx_len % KEY_TILE == 0 and ctx_len % Q_SUB == 0, (seq, ctx_len)

    def lat_idx(b, i):
        return (b * n_lat_steps + jnp.minimum(i, n_lat_steps - 1), 0)

    return pl.pallas_call(
        functools.partial(_attn_kernel, n_lat_steps=n_lat_steps),
        grid=(batch, n_lat_steps + 1),
        in_specs=[
            pl.BlockSpec((Q_TILE, attn_dim), lat_idx),
            pl.BlockSpec((ctx_len, attn_dim), lambda b, i: (lat_rows // ctx_len + b, 0)),
            pl.BlockSpec((seq, KV_DIM), lambda b, i: (b, 0)),
            pl.BlockSpec((ctx_len, KV_DIM), lambda b, i: (lat_rows // ctx_len + b, 0)),
            pl.BlockSpec((seq // KEY_TILE, vt_rows, KEY_TILE), lambda b, i: (b, 0, 0)),
            pl.BlockSpec((ctx_len // KEY_TILE, vt_rows, KEY_TILE), lambda b, i: (lat_rows // ctx_len + b, 0, 0)),
        ],
        out_specs=[
            pl.BlockSpec((Q_TILE, attn_dim), lat_idx),
            pl.BlockSpec((ctx_len, attn_dim), lambda b, i: (b, 0)),
        ],
        out_shape=[
            jax.ShapeDtypeStruct((lat_rows, attn_dim), BF16),
            jax.ShapeDtypeStruct((rows - lat_rows, attn_dim), BF16),
        ],
        scratch_shapes=[pltpu.VMEM((SCORE_SLOTS, N_KV_HEADS, KEY_TILE, Q_PER_KV * Q_SUB), F32)],
        compiler_params=_params("arbitrary", "arbitrary"),
        name="attention",
    )(q, q, k, k, vt, vt)


def _out_ffn_kernel(x_ref, atl_ref, atc_ref, gm_ref, mod_ref, g2_ref, wo_ref, w1_ref, w2_ref, o_ref, *, d_model,
                    lat_tiles):
    mod = mod_ref[...]
    gate1 = mod[:, 2 * d_model:3 * d_model]
    shift = mod[:, 3 * d_model:4 * d_model]
    scale = mod[:, 4 * d_model:5 * d_model]
    gate2 = mod[:, 5 * d_model:6 * d_model]
    attn_dim = atl_ref.shape[1]
    at = jnp.where(pl.program_id(0) < lat_tiles, atl_ref[...], atc_ref[...])
    y = _dot(at, wo_ref[0:attn_dim, :]) + _dot(gm_ref[...], wo_ref[attn_dim:, :])
    x1 = x_ref[...] + gate1 * y
    ms = jnp.mean(x1 * x1, axis=-1, keepdims=True)
    h = (x1 * lax.rsqrt(ms + EPS)) * g2_ref[...]
    hb = (h * (1.0 + scale) + shift).astype(BF16)
    acc = None
    for c in range(w1_ref.shape[1] // FF_TILE):
        cols = slice(c * FF_TILE, (c + 1) * FF_TILE)
        t = jnp.maximum(_dot(hb, w1_ref[:, cols]), 0.0)
        part = _dot((t * t).astype(BF16), w2_ref[cols, :])
        acc = part if acc is None else acc + part
    o_ref[...] = x1 + gate2 * acc


def _out_ffn(layer, xs, attn_lat, attn_ctx, gm, mods, g2, w_out, w_ff1, w_ff2, *, n_tiles, lat_tiles,
             tiles_per_batch):
    d = xs.shape[1]
    attn_dim = attn_lat.shape[1]
    ff = w_ff1.shape[-1]
    ctx_mod_row = lat_tiles // tiles_per_batch

    def mod_idx(j):
        return (layer, jnp.where(j < lat_tiles, j // tiles_per_batch, ctx_mod_row), 0, 0)

    resident = pl.Buffered(1)
    kern = functools.partial(_out_ffn_kernel, d_model=d, lat_tiles=lat_tiles)
    return pl.pallas_call(
        kern,
        grid=(n_tiles,),
        in_specs=[
            pl.BlockSpec((ROW_TILE, d), lambda j: (j, 0)),
            pl.BlockSpec((ROW_TILE, attn_dim), lambda j: (jnp.minimum(j, lat_tiles - 1), 0)),
            pl.BlockSpec((ROW_TILE, attn_dim), lambda j: (jnp.maximum(j - lat_tiles, 0), 0)),
            pl.BlockSpec((ROW_TILE, gm.shape[1]), lambda j: (j, 0)),
            pl.BlockSpec((None, None, 1, mods.shape[-1]), mod_idx),
            pl.BlockSpec((1, d), lambda j: (0, 0)),
            pl.BlockSpec(w_out.shape, lambda j: (0, 0), pipeline_mode=resident),
            pl.BlockSpec((d, ff), lambda j: (0, 0), pipeline_mode=resident),
            pl.BlockSpec((ff, d), lambda j: (0, 0), pipeline_mode=resident),
        ],
        out_specs=pl.BlockSpec((ROW_TILE, d), lambda j: (j, 0)),
        out_shape=jax.ShapeDtypeStruct((n_tiles * ROW_TILE, d), F32),
        compiler_params=_params("arbitrary"),
        name="out_ffn",
    )(xs, attn_lat, attn_ctx, gm, mods, g2, w_out, w_ff1, w_ff2)


def _rope_tables(seq):
    n_rows = seq // GRID_W
    n_freq = HEAD_DIM // 4
    inv_freq = ROPE_THETA ** (-jnp.arange(n_freq, dtype=F32) / n_freq)
    ang_row = jnp.arange(n_rows, dtype=F32)[:, None] * inv_freq
    ang_col = jnp.arange(GRID_W, dtype=F32)[:, None] * inv_freq

    def table(fn, sign):
        by_row = jnp.repeat(fn(ang_row), GRID_W, axis=0)
        by_col = jnp.tile(fn(ang_col), (n_rows, 1))
        head = jnp.concatenate([sign[0] * by_row, sign[1] * by_row, sign[0] * by_col, sign[1] * by_col], axis=1)
        return jnp.tile(head, (1, LANES // HEAD_DIM))

    cos_t = jnp.concatenate([table(jnp.cos, (1.0, 1.0)), jnp.ones((ROW_TILE, LANES), F32)], axis=0)
    sin_t = jnp.concatenate([table(jnp.sin, (-1.0, 1.0)), jnp.zeros((ROW_TILE, LANES), F32)], axis=0)
    return cos_t, sin_t


def _head_indicator(width):
    idx = np.arange(width) // HEAD_DIM
    return jnp.asarray((idx[:, None] == idx[None, :]).astype(np.float32), dtype=BF16)


def kernel(x, c, ctx, c_ctx, w_mod, b_mod, norm1_g, w_in, q_norm_g, k_norm_g, gmlp_norm_g, w_spatial, b_spatial,
           w_out, norm2_g, w_ff1, w_ff2):
    batch, seq, d = x.shape
    ctx_len = ctx.shape[1]
    depth = w_mod.shape[0]
    gmlp_dim = gmlp_norm_g.shape[-1]
    attn_dim = w_out.shape[1] - gmlp_dim
    lat_rows = batch * seq
    lat_tiles = lat_rows // ROW_TILE
    tiles_per_batch = seq // ROW_TILE

    cond = jnp.concatenate([c, c_ctx[None, :], jnp.zeros((MOD_ROWS - batch - 1, d), F32)], axis=0)
    mods = _modulation(cond, w_mod, b_mod).reshape(depth, MOD_ROWS, 1, N_MOD * d)

    cos_t, sin_t = _rope_tables(seq)
    s_q = _head_indicator(attn_dim)
    s_k = _head_indicator(KV_DIM)
    w_in_b = w_in.astype(BF16)
    w_sp_b = w_spatial.astype(BF16)

    x_srcs = (x.reshape(lat_rows, d), ctx.reshape(batch * ctx_len, d))
    for l in range(depth):
        last = l == depth - 1
        outs = _inproj(
            l, x_srcs, mods, norm1_g[l][None, :], w_in_b,
            jnp.tile(q_norm_g[l], attn_dim // HEAD_DIM)[None, :],
            jnp.tile(k_norm_g[l], KV_DIM // HEAD_DIM)[None, :],
            cos_t, sin_t, s_q, s_k, gmlp_norm_g[l][None, :], w_sp_b[l], b_spatial[l].T, w_out, w_ff1, w_ff2,
            lat_tiles=lat_tiles, tiles_per_batch=tiles_per_batch)
        q, k, vt, gm, w_out_b, w_ff1_b, w_ff2_b = outs[:7]
        xs = outs[7] if len(x_srcs) == 2 else x_srcs[0]
        attn_lat, attn_ctx = _attention(q, k, vt, batch=batch, seq=seq, ctx_len=ctx_len)
        n_tiles = lat_tiles if last else xs.shape[0] // ROW_TILE
        xs = _out_ffn(l, xs, attn_lat, attn_ctx, gm, mods, norm2_g[l][None, :], w_out_b, w_ff1_b, w_ff2_b,
                      n_tiles=n_tiles, lat_tiles=lat_tiles, tiles_per_batch=tiles_per_batch)
        x_srcs = (xs,)
    return xs.reshape(batch, seq, d)
```

```python
import functools

import numpy as np
import jax
import jax.numpy as jnp
from jax import lax
from jax.experimental import pallas as pl
from jax.experimental.pallas import tpu as pltpu

F32 = jnp.float32
BF16 = jnp.bfloat16

HEAD_DIM = 64
Q_PER_KV = 4
N_KV_HEADS = 2
KV_DIM = N_KV_HEADS * HEAD_DIM
GROUP_DIM = 128
N_GROUPS = 4
GRID_W = 64
ROPE_THETA = 10000.0
EPS = 1e-6
N_MOD = 6
Q_SCALE = HEAD_DIM ** -0.5
LOG2_E = 1.4426950408889634

LANES = 128
ROW_TILE = 512
KEY_TILE = 256
SCORE_LOOKAHEAD = 2
SCORE_SLOTS = SCORE_LOOKAHEAD // 2 + 2
SUM_ROWS = 16
V_ROWS = HEAD_DIM + SUM_ROWS
Q_SUB = 128
Q_TILE = 512
FF_TILE = 1024
MOD_COL_TILE = 2048
MOD_ROWS = 8
VMEM_LIMIT = 56 * 1024 * 1024


def _dot(a, b):
    return jnp.dot(a, b, preferred_element_type=F32)


def _gelu_tanh(x):
    c = np.sqrt(2.0 / np.pi).astype(np.float32)
    inner = x * ((x * x) * (c * np.float32(0.044715)) + c)
    half = 0.5 * x
    return half + half * jnp.tanh(inner)


def _split_bf16(t):
    hi = t.astype(BF16)
    lo = (t - hi.astype(F32)).astype(BF16)
    return hi, lo


def _params(*sem):
    return pltpu.CompilerParams(dimension_semantics=sem, vmem_limit_bytes=VMEM_LIMIT)


def _mod_kernel(cond_ref, w_ref, b_ref, o_ref):
    cnd = cond_ref[...]
    act = cnd * (1.0 / (1.0 + jnp.exp(-cnd)))
    a_hi, a_lo = _split_bf16(act)
    w_hi, w_lo = _split_bf16(w_ref[...])
    acc = _dot(a_hi, w_hi) + _dot(a_lo, w_hi) + _dot(a_hi, w_lo)
    o_ref[...] = acc + b_ref[...]


def _modulation(cond, w_mod, b_mod):
    depth, d, n_out = w_mod.shape
    return pl.pallas_call(
        _mod_kernel,
        grid=(depth, n_out // MOD_COL_TILE),
        in_specs=[
            pl.BlockSpec((MOD_ROWS, d), lambda l, j: (0, 0)),
            pl.BlockSpec((None, d, MOD_COL_TILE), lambda l, j: (l, 0, j)),
            pl.BlockSpec((None, 1, MOD_COL_TILE), lambda l, j: (l, 0, j)),
        ],
        out_specs=pl.BlockSpec((None, MOD_ROWS, MOD_COL_TILE), lambda l, j: (l, 0, j)),
        out_shape=jax.ShapeDtypeStruct((depth, MOD_ROWS, n_out), F32),
        compiler_params=_params("arbitrary", "arbitrary"),
        name="modulation",
    )(cond, w_mod, b_mod.reshape(depth, 1, n_out))


def _inproj_kernel(*refs, d_model, attn_dim, gmlp_dim, lat_tiles, split_input, n_casts):
    n_x = 2 if split_input else 1
    x_refs, refs = refs[:n_x], refs[n_x:]
    (mod_ref, g1_ref, w_ref, qg_ref, kg_ref, cos_ref, sin_ref, s_q_ref, s_k_ref, gng_ref, ws_ref, bs_ref,
     *refs) = refs
    cast_src, (q_ref, k_ref, vt_ref, gm_ref, *refs) = refs[:n_casts], refs[n_casts:]
    cast_dst, xs_refs = refs[:n_casts], refs[n_casts:]
    if split_input:
        x = jnp.where(pl.program_id(0) < lat_tiles, x_refs[0][...], x_refs[1][...])
        xs_refs[0][...] = x
    else:
        x = x_refs[0][...]
    tm = x.shape[0]
    for src, dst in zip(cast_src, cast_dst):
        dst[...] = src[...].astype(BF16)
    mod = mod_ref[...]
    shift, scale = mod[:, 0:d_model], mod[:, d_model:2 * d_model]
    ms = jnp.mean(x * x, axis=-1, keepdims=True)
    h = (x * lax.rsqrt(ms + EPS)) * g1_ref[...]
    hb = (h * (1.0 + scale) + shift).astype(BF16)

    cos, sin = cos_ref[...], sin_ref[...]
    lane = lax.broadcasted_iota(jnp.int32, (tm, LANES), 1)
    second_half = (lane & (HEAD_DIM // 4)) != 0

    def rope(t):
        partner = jnp.where(second_half, pltpu.roll(t, HEAD_DIM // 4, 1),
                            pltpu.roll(t, LANES - HEAD_DIM // 4, 1))
        return t * cos + partner * sin

    def head_mean_sq(t, s_ref):
        return _dot((t * t).astype(BF16), s_ref[...]) * (1.0 / HEAD_DIM)

    kv_lo = attn_dim
    u_lo = attn_dim + 2 * KV_DIM
    g_lo = u_lo + gmlp_dim

    gate = _gelu_tanh(_dot(hb, w_ref[:, g_lo:g_lo + gmlp_dim]))
    zq = _dot(hb, w_ref[:, 0:attn_dim])
    zkv = _dot(hb, w_ref[:, kv_lo:u_lo])

    vn = []
    for g in range(N_GROUPS):
        cols = slice(g * GROUP_DIM, (g + 1) * GROUP_DIM)
        t = gate[:, cols]
        dev = t - jnp.mean(t, axis=-1, keepdims=True)
        var = jnp.mean(dev * dev, axis=-1, keepdims=True)
        vn.append(((dev * lax.rsqrt(var + EPS)) * gng_ref[:, cols]).astype(BF16))

    qn = (zq * lax.rsqrt(head_mean_sq(zq, s_q_ref) + EPS)) * qg_ref[...]
    zk, zv = zkv[:, 0:KV_DIM], zkv[:, KV_DIM:2 * KV_DIM]
    kn = (zk * lax.rsqrt(head_mean_sq(zk, s_k_ref) + EPS)) * kg_ref[...]
    u = _gelu_tanh(_dot(hb, w_ref[:, u_lo:g_lo]))

    for j in range(attn_dim // LANES):
        cols = slice(j * LANES, (j + 1) * LANES)
        q_ref[:, cols] = (rope(qn[:, cols]) * (Q_SCALE * LOG2_E)).astype(BF16)
    k_ref[...] = rope(kn).astype(BF16)
    vt = zv.T
    ones = jnp.ones((SUM_ROWS, KEY_TILE), BF16)
    for c in range(tm // KEY_TILE):
        keys = slice(c * KEY_TILE, (c + 1) * KEY_TILE)
        for kvh in range(N_KV_HEADS):
            vt_ref[c, kvh * V_ROWS:kvh * V_ROWS + HEAD_DIM, :] = vt[kvh * HEAD_DIM:(kvh + 1) * HEAD_DIM, keys].astype(BF16)
            vt_ref[c, kvh * V_ROWS + HEAD_DIM:(kvh + 1) * V_ROWS, :] = ones

    for g in range(N_GROUPS):
        cols = slice(g * GROUP_DIM, (g + 1) * GROUP_DIM)
        for c in range(tm // GROUP_DIM):
            rows = slice(c * GROUP_DIM, (c + 1) * GROUP_DIM)
            mixed = _dot(ws_ref[g], vn[g][rows, :]) + bs_ref[:, g:g + 1]
            gm_ref[rows, cols] = (u[rows, cols] * mixed).astype(BF16)


def _inproj(layer, x_srcs, mods, g1, w_in, qg, kg, cos_t, sin_t, s_q, s_k, gng, ws, bs, cast_ws, *,
            lat_tiles, tiles_per_batch):
    split_input = len(x_srcs) == 2
    rows = sum(a.shape[0] for a in x_srcs)
    d = x_srcs[0].shape[1]
    in_dim = w_in.shape[-1]
    attn_dim = qg.shape[-1]
    gmlp_dim = gng.shape[-1]
    n_tiles = rows // ROW_TILE
    ctx_mod_row = lat_tiles // tiles_per_batch
    ident_tile = cos_t.shape[0] // ROW_TILE - 1

    def mod_idx(j):
        return (layer, jnp.where(j < lat_tiles, j // tiles_per_batch, ctx_mod_row), 0, 0)

    def rope_idx(j):
        return (jnp.where(j < lat_tiles, j % tiles_per_batch, ident_tile), 0)

    const2 = lambda j: (0, 0)
    row_spec = pl.BlockSpec((ROW_TILE, d), lambda j: (j, 0))
    if split_input:
        x_specs = [pl.BlockSpec((ROW_TILE, d), lambda j: (jnp.minimum(j, lat_tiles - 1), 0)),
                   pl.BlockSpec((ROW_TILE, d), lambda j: (jnp.maximum(j - lat_tiles, 0), 0))]
    else:
        x_specs = [row_spec]

    def cast_specs(w, w_layer):
        blk = (w.shape[1] // lat_tiles, w.shape[2])
        return (pl.BlockSpec((None,) + blk, lambda j: (w_layer, jnp.minimum(j, lat_tiles - 1), 0)),
                pl.BlockSpec(blk, lambda j: (jnp.minimum(j, lat_tiles - 1), 0)),
                jax.ShapeDtypeStruct(w.shape[1:], BF16))

    casts = [cast_specs(w, w_layer) for w, w_layer in cast_ws]
    kern = functools.partial(_inproj_kernel, d_model=d, attn_dim=attn_dim, gmlp_dim=gmlp_dim,
                             lat_tiles=lat_tiles, split_input=split_input, n_casts=len(casts))
    return pl.pallas_call(
        kern,
        grid=(n_tiles,),
        in_specs=x_specs + [
            pl.BlockSpec((None, None, 1, mods.shape[-1]), mod_idx),
            pl.BlockSpec((1, d), const2),
            pl.BlockSpec((d, in_dim), const2),
            pl.BlockSpec((1, attn_dim), const2),
            pl.BlockSpec((1, KV_DIM), const2),
            pl.BlockSpec((ROW_TILE, LANES), rope_idx),
            pl.BlockSpec((ROW_TILE, LANES), rope_idx),
            pl.BlockSpec(s_q.shape, const2),
            pl.BlockSpec(s_k.shape, const2),
            pl.BlockSpec((1, gmlp_dim), const2),
            pl.BlockSpec(ws.shape, lambda j: (0, 0, 0)),
            pl.BlockSpec(bs.shape, const2),
        ] + [c[0] for c in casts],
        out_specs=[
            pl.BlockSpec((ROW_TILE, attn_dim), lambda j: (j, 0)),
            pl.BlockSpec((ROW_TILE, KV_DIM), lambda j: (j, 0)),
            pl.BlockSpec((ROW_TILE // KEY_TILE, N_KV_HEADS * V_ROWS, KEY_TILE), lambda j: (j, 0, 0)),
            pl.BlockSpec((ROW_TILE, gmlp_dim), lambda j: (j, 0)),
        ] + [c[1] for c in casts] + ([row_spec] if split_input else []),
        out_shape=[
            jax.ShapeDtypeStruct((rows, attn_dim), BF16),
            jax.ShapeDtypeStruct((rows, KV_DIM), BF16),
            jax.ShapeDtypeStruct((rows // KEY_TILE, N_KV_HEADS * V_ROWS, KEY_TILE), BF16),
            jax.ShapeDtypeStruct((rows, gmlp_dim), BF16),
        ] + [c[2] for c in casts] + ([jax.ShapeDtypeStruct((rows, d), F32)] if split_input else []),
        compiler_params=_params("arbitrary"),
        name="inproj",
    )(*x_srcs, mods, g1, w_in, qg, kg, cos_t, sin_t, s_q, s_k, gng, ws, bs, *[w for w, _ in cast_ws])


def _attn_kernel(ql_ref, qc_ref, kl_ref, kc_ref, vtl_ref, vtc_ref, ol_ref, oc_ref, s_ref, *, n_lat_steps):
    tq = Q_SUB
    n_cols = Q_PER_KV * tq
    group_w = Q_PER_KV * HEAD_DIM
    heads = range(N_KV_HEADS)
    refs = {"ctx": (kc_ref, vtc_ref), "lat": (kl_ref, vtl_ref)}

    def key_slabs(kinds):
        return [(kind, j) for kind in kinds for j in range(refs[kind][1].shape[0])]

    def stacked_qt(q_ref, sub, kvh):
        rows = slice(sub * tq, (sub + 1) * tq)
        qt = q_ref[rows, kvh * group_w:(kvh + 1) * group_w].astype(F32).T
        qst = jnp.concatenate([qt[h * HEAD_DIM:(h + 1) * HEAD_DIM, :] for h in range(Q_PER_KV)], axis=1)
        parts = [jnp.zeros_like(qst)] * N_KV_HEADS
        parts[kvh] = qst
        return jnp.concatenate(parts, axis=0).astype(BF16)

    def put_scores(slot, slab, kvh, qst):
        kind, j = slab
        s = _dot(refs[kind][0][j * KEY_TILE:(j + 1) * KEY_TILE, :], qst)
        s_ref[slot, kvh] = s
        return jnp.max(s, axis=0, keepdims=True)

    def update(slot, slab, kvh, s_max, carry):
        kind, j = slab
        m, acc = carry
        m_new = jnp.maximum(m, s_max)
        alpha = jnp.exp2(m - m_new)
        p = jnp.exp2(s_ref[slot, kvh] - m_new)
        vt = refs[kind][1][j, kvh * V_ROWS:(kvh + 1) * V_ROWS, :]
        return m_new, alpha * acc + _dot(vt, p.astype(BF16))

    def finish(o_ref, sub, kvh, carry):
        acc = carry[1]
        ot = acc[0:HEAD_DIM, :] / acc[HEAD_DIM:HEAD_DIM + 1, :]
        ot = jnp.concatenate([ot[:, h * tq:(h + 1) * tq] for h in range(Q_PER_KV)], axis=0)
        o_ref[sub * tq:(sub + 1) * tq, kvh * group_w:(kvh + 1) * group_w] = ot.T.astype(BF16)

    def attend(q_ref, o_ref, slabs):
        n_sub = q_ref.shape[0] // tq
        items = [(sub, slab, kvh) for sub in range(n_sub) for slab in slabs for kvh in heads]
        qst, s_max, carry = {}, {}, {}

        assert SCORE_LOOKAHEAD % N_KV_HEADS == 0

        def slot(u):
            return (u // N_KV_HEADS) % SCORE_SLOTS

        def put(u):
            sub, slab, kvh = items[u]
            if (sub, kvh) not in qst:
                qst[sub, kvh] = stacked_qt(q_ref, sub, kvh)
            s_max[u] = put_scores(slot(u), slab, kvh, qst[sub, kvh])

        for u in range(min(SCORE_LOOKAHEAD, len(items))):
            put(u)
        for u, (sub, slab, kvh) in enumerate(items):
            if u + SCORE_LOOKAHEAD < len(items):
                put(u + SCORE_LOOKAHEAD)
            if slab is slabs[0]:
                carry[kvh] = (jnp.full((1, n_cols), -jnp.inf, F32), jnp.zeros((V_ROWS, n_cols), F32))
            carry[kvh] = update(slot(u), slab, kvh, s_max.pop(u), carry[kvh])
            if slab is slabs[-1]:
                finish(o_ref, sub, kvh, carry[kvh])

    i = pl.program_id(1)
    pl.when(i < n_lat_steps)(lambda: attend(ql_ref, ol_ref, key_slabs(("ctx", "lat"))))
    pl.when(i >= n_lat_steps)(lambda: attend(qc_ref, oc_ref, key_slabs(("ctx",))))


def _attention(q, k, vt, *, batch, seq, ctx_len):
    rows, attn_dim = q.shape
    lat_rows = batch * seq
    n_lat_steps = seq // Q_TILE
    vt_rows = vt.shape[1]
    assert seq % KEY_TILE == 0 and ctx_len % KEY_TILE == 0 and ctx_len % Q_SUB == 0, (seq, ctx_len)

    def lat_idx(b, i):
        return (b * n_lat_steps + jnp.minimum(i, n_lat_steps - 1), 0)

    return pl.pallas_call(
        functools.partial(_attn_kernel, n_lat_steps=n_lat_steps),
        grid=(batch, n_lat_steps + 1),
        in_specs=[
            pl.BlockSpec((Q_TILE, attn_dim), lat_idx),
            pl.BlockSpec((ctx_len, attn_dim), lambda b, i: (lat_rows // ctx_len + b, 0)),
            pl.BlockSpec((seq, KV_DIM), lambda b, i: (b, 0)),
            pl.BlockSpec((ctx_len, KV_DIM), lambda b, i: (lat_rows // ctx_len + b, 0)),
            pl.BlockSpec((seq // KEY_TILE, vt_rows, KEY_TILE), lambda b, i: (b, 0, 0)),
            pl.BlockSpec((ctx_len // KEY_TILE, vt_rows, KEY_TILE), lambda b, i: (lat_rows // ctx_len + b, 0, 0)),
        ],
        out_specs=[
            pl.BlockSpec((Q_TILE, attn_dim), lat_idx),
            pl.BlockSpec((ctx_len, attn_dim), lambda b, i: (b, 0)),
        ],
        out_shape=[
            jax.ShapeDtypeStruct((lat_rows, attn_dim), BF16),
            jax.ShapeDtypeStruct((rows - lat_rows, attn_dim), BF16),
        ],
        scratch_shapes=[pltpu.VMEM((SCORE_SLOTS, N_KV_HEADS, KEY_TILE, Q_PER_KV * Q_SUB), F32)],
        compiler_params=_params("arbitrary", "arbitrary"),
        name="attention",
    )(q, q, k, k, vt, vt)


def _out_ffn_kernel(x_ref, atl_ref, atc_ref, gm_ref, mod_ref, g2_ref, wo_ref, w1_ref, w2_ref, o_ref, *, d_model,
                    lat_tiles):
    mod = mod_ref[...]
    gate1 = mod[:, 2 * d_model:3 * d_model]
    shift = mod[:, 3 * d_model:4 * d_model]
    scale = mod[:, 4 * d_model:5 * d_model]
    gate2 = mod[:, 5 * d_model:6 * d_model]
    attn_dim = atl_ref.shape[1]
    at = jnp.where(pl.program_id(0) < lat_tiles, atl_ref[...], atc_ref[...])
    y = _dot(at, wo_ref[0:attn_dim, :]) + _dot(gm_ref[...], wo_ref[attn_dim:, :])
    x1 = x_ref[...] + gate1 * y
    ms = jnp.mean(x1 * x1, axis=-1, keepdims=True)
    h = (x1 * lax.rsqrt(ms + EPS)) * g2_ref[...]
    hb = (h * (1.0 + scale) + shift).astype(BF16)
    acc = None
    for c in range(w1_ref.shape[1] // FF_TILE):
        cols = slice(c * FF_TILE, (c + 1) * FF_TILE)
        t = jnp.maximum(_dot(hb, w1_ref[:, cols]), 0.0)
        part = _dot((t * t).astype(BF16), w2_ref[cols, :])
        acc = part if acc is None else acc + part
    o_ref[...] = x1 + gate2 * acc


def _out_ffn(layer, xs, attn_lat, attn_ctx, gm, mods, g2, w_out, w_ff1, w_ff2, *, n_tiles, lat_tiles,
             tiles_per_batch):
    d = xs.shape[1]
    attn_dim = attn_lat.shape[1]
    ff = w_ff1.shape[-1]
    ctx_mod_row = lat_tiles // tiles_per_batch

    def mod_idx(j):
        return (layer, jnp.where(j < lat_tiles, j // tiles_per_batch, ctx_mod_row), 0, 0)

    resident = pl.Buffered(1)
    kern = functools.partial(_out_ffn_kernel, d_model=d, lat_tiles=lat_tiles)
    return pl.pallas_call(
        kern,
        grid=(n_tiles,),
        in_specs=[
            pl.BlockSpec((ROW_TILE, d), lambda j: (j, 0)),
            pl.BlockSpec((ROW_TILE, attn_dim), lambda j: (jnp.minimum(j, lat_tiles - 1), 0)),
            pl.BlockSpec((ROW_TILE, attn_dim), lambda j: (jnp.maximum(j - lat_tiles, 0), 0)),
            pl.BlockSpec((ROW_TILE, gm.shape[1]), lambda j: (j, 0)),
            pl.BlockSpec((None, None, 1, mods.shape[-1]), mod_idx),
            pl.BlockSpec((1, d), lambda j: (0, 0)),
            pl.BlockSpec(w_out.shape, lambda j: (0, 0), pipeline_mode=resident),
            pl.BlockSpec((d, ff), lambda j: (0, 0), pipeline_mode=resident),
            pl.BlockSpec((ff, d), lambda j: (0, 0), pipeline_mode=resident),
        ],
        out_specs=pl.BlockSpec((ROW_TILE, d), lambda j: (j, 0)),
        out_shape=jax.ShapeDtypeStruct((n_tiles * ROW_TILE, d), F32),
        compiler_params=_params("arbitrary"),
        name="out_ffn",
    )(xs, attn_lat, attn_ctx, gm, mods, g2, w_out, w_ff1, w_ff2)


def _rope_tables(seq):
    n_rows = seq // GRID_W
    n_freq = HEAD_DIM // 4
    inv_freq = ROPE_THETA ** (-jnp.arange(n_freq, dtype=F32) / n_freq)
    ang_row = jnp.arange(n_rows, dtype=F32)[:, None] * inv_freq
    ang_col = jnp.arange(GRID_W, dtype=F32)[:, None] * inv_freq

    def table(fn, sign):
        by_row = jnp.repeat(fn(ang_row), GRID_W, axis=0)
        by_col = jnp.tile(fn(ang_col), (n_rows, 1))
        head = jnp.concatenate([sign[0] * by_row, sign[1] * by_row, sign[0] * by_col, sign[1] * by_col], axis=1)
        return jnp.tile(head, (1, LANES // HEAD_DIM))

    cos_t = jnp.concatenate([table(jnp.cos, (1.0, 1.0)), jnp.ones((ROW_TILE, LANES), F32)], axis=0)
    sin_t = jnp.concatenate([table(jnp.sin, (-1.0, 1.0)), jnp.zeros((ROW_TILE, LANES), F32)], axis=0)
    return cos_t, sin_t


def _head_indicator(width):
    idx = np.arange(width) // HEAD_DIM
    return jnp.asarray((idx[:, None] == idx[None, :]).astype(np.float32), dtype=BF16)


def kernel(x, c, ctx, c_ctx, w_mod, b_mod, norm1_g, w_in, q_norm_g, k_norm_g, gmlp_norm_g, w_spatial, b_spatial,
           w_out, norm2_g, w_ff1, w_ff2):
    batch, seq, d = x.shape
    ctx_len = ctx.shape[1]
    depth = w_mod.shape[0]
    gmlp_dim = gmlp_norm_g.shape[-1]
    attn_dim = w_out.shape[1] - gmlp_dim
    lat_rows = batch * seq
    lat_tiles = lat_rows // ROW_TILE
    tiles_per_batch = seq // ROW_TILE

    cond = jnp.concatenate([c, c_ctx[None, :], jnp.zeros((MOD_ROWS - batch - 1, d), F32)], axis=0)
    mods = _modulation(cond, w_mod, b_mod).reshape(depth, MOD_ROWS, 1, N_MOD * d)

    cos_t, sin_t = _rope_tables(seq)
    s_q = _head_indicator(attn_dim)
    s_k = _head_indicator(KV_DIM)
    w_in_b = w_in[0].astype(BF16)
    w_sp_b = w_spatial.astype(BF16)

    x_srcs = (x.reshape(lat_rows, d), ctx.reshape(batch * ctx_len, d))
    for l in range(depth):
        last = l == depth - 1
        cast_ws = [(w_out, l), (w_ff1, l), (w_ff2, l)] + ([] if last else [(w_in, l + 1)])
        outs = _inproj(
            l, x_srcs, mods, norm1_g[l][None, :], w_in_b,
            jnp.tile(q_norm_g[l], attn_dim // HEAD_DIM)[None, :],
            jnp.tile(k_norm_g[l], KV_DIM // HEAD_DIM)[None, :],
            cos_t, sin_t, s_q, s_k, gmlp_norm_g[l][None, :], w_sp_b[l], b_spatial[l].T, cast_ws,
            lat_tiles=lat_tiles, tiles_per_batch=tiles_per_batch)
        q, k, vt, gm, w_out_b, w_ff1_b, w_ff2_b = outs[:7]
        outs = outs[7:]
        if not last:
            w_in_b, outs = outs[0], outs[1:]
        xs = outs[0] if len(x_srcs) == 2 else x_srcs[0]
        attn_lat, attn_ctx = _attention(q, k, vt, batch=batch, seq=seq, ctx_len=ctx_len)
        n_tiles = lat_tiles if last else xs.shape[0] // ROW_TILE
        xs = _out_ffn(l, xs, attn_lat, attn_ctx, gm, mods, norm2_g[l][None, :], w_out_b, w_ff1_b, w_ff2_b,
                      n_tiles=n_tiles, lat_tiles=lat_tiles, tiles_per_batch=tiles_per_batch)
        x_srcs = (xs,)
    return xs.reshape(batch, seq, d)
```

```python
import functools

import numpy as np
import jax
import jax.numpy as jnp
from jax import lax
from jax.experimental import pallas as pl
from jax.experimental.pallas import tpu as pltpu

F32 = jnp.float32
BF16 = jnp.bfloat16

HEAD_DIM = 64
Q_PER_KV = 4
N_KV_HEADS = 2
KV_DIM = N_KV_HEADS * HEAD_DIM
GROUP_DIM = 128
N_GROUPS = 4
GRID_W = 64
ROPE_THETA = 10000.0
EPS = 1e-6
N_MOD = 6
Q_SCALE = HEAD_DIM ** -0.5
LOG2_E = 1.4426950408889634

LANES = 128
ROW_TILE = 512
KEY_TILE = 256
SCORE_LOOKAHEAD = 2
SCORE_SLOTS = SCORE_LOOKAHEAD // 2 + 1
SUM_ROWS = 16
V_ROWS = HEAD_DIM + SUM_ROWS
Q_SUB = 128
Q_TILE = 512
FF_TILE = 1024
MOD_COL_TILE = 2048
MOD_ROWS = 8
VMEM_LIMIT = 56 * 1024 * 1024


def _dot(a, b):
    return jnp.dot(a, b, preferred_element_type=F32)


def _gelu_tanh(x):
    c = np.sqrt(2.0 / np.pi).astype(np.float32)
    inner = x * ((x * x) * (c * np.float32(0.044715)) + c)
    half = 0.5 * x
    return half + half * jnp.tanh(inner)


def _split_bf16(t):
    hi = t.astype(BF16)
    lo = (t - hi.astype(F32)).astype(BF16)
    return hi, lo


def _params(*sem):
    return pltpu.CompilerParams(dimension_semantics=sem, vmem_limit_bytes=VMEM_LIMIT)


def _mod_kernel(cond_ref, w_ref, b_ref, o_ref):
    cnd = cond_ref[...]
    act = cnd * (1.0 / (1.0 + jnp.exp(-cnd)))
    a_hi, a_lo = _split_bf16(act)
    w_hi, w_lo = _split_bf16(w_ref[...])
    acc = _dot(a_hi, w_hi) + _dot(a_lo, w_hi) + _dot(a_hi, w_lo)
    o_ref[...] = acc + b_ref[...]


def _modulation(cond, w_mod, b_mod):
    depth, d, n_out = w_mod.shape
    return pl.pallas_call(
        _mod_kernel,
        grid=(depth, n_out // MOD_COL_TILE),
        in_specs=[
            pl.BlockSpec((MOD_ROWS, d), lambda l, j: (0, 0)),
            pl.BlockSpec((None, d, MOD_COL_TILE), lambda l, j: (l, 0, j)),
            pl.BlockSpec((None, 1, MOD_COL_TILE), lambda l, j: (l, 0, j)),
        ],
        out_specs=pl.BlockSpec((None, MOD_ROWS, MOD_COL_TILE), lambda l, j: (l, 0, j)),
        out_shape=jax.ShapeDtypeStruct((depth, MOD_ROWS, n_out), F32),
        compiler_params=_params("arbitrary", "arbitrary"),
        name="modulation",
    )(cond, w_mod, b_mod.reshape(depth, 1, n_out))


def _inproj_kernel(*refs, d_model, attn_dim, gmlp_dim, lat_tiles, split_input, n_casts):
    n_x = 2 if split_input else 1
    x_refs, refs = refs[:n_x], refs[n_x:]
    (mod_ref, g1_ref, w_ref, qg_ref, kg_ref, cos_ref, sin_ref, s_q_ref, s_k_ref, gng_ref, ws_ref, bs_ref,
     *refs) = refs
    cast_src, (q_ref, k_ref, vt_ref, gm_ref, *refs) = refs[:n_casts], refs[n_casts:]
    cast_dst, xs_refs = refs[:n_casts], refs[n_casts:]
    if split_input:
        x = jnp.where(pl.program_id(0) < lat_tiles, x_refs[0][...], x_refs[1][...])
        xs_refs[0][...] = x
    else:
        x = x_refs[0][...]
    tm = x.shape[0]
    for src, dst in zip(cast_src, cast_dst):
        dst[...] = src[...].astype(BF16)
    mod = mod_ref[...]
    shift, scale = mod[:, 0:d_model], mod[:, d_model:2 * d_model]
    ms = jnp.mean(x * x, axis=-1, keepdims=True)
    h = (x * lax.rsqrt(ms + EPS)) * g1_ref[...]
    hb = (h * (1.0 + scale) + shift).astype(BF16)

    cos, sin = cos_ref[...], sin_ref[...]
    lane = lax.broadcasted_iota(jnp.int32, (tm, LANES), 1)
    second_half = (lane & (HEAD_DIM // 4)) != 0

    def rope(t):
        partner = jnp.where(second_half, pltpu.roll(t, HEAD_DIM // 4, 1),
                            pltpu.roll(t, LANES - HEAD_DIM // 4, 1))
        return t * cos + partner * sin

    def head_mean_sq(t, s_ref):
        return _dot((t * t).astype(BF16), s_ref[...]) * (1.0 / HEAD_DIM)

    kv_lo = attn_dim
    u_lo = attn_dim + 2 * KV_DIM
    g_lo = u_lo + gmlp_dim

    gate = _gelu_tanh(_dot(hb, w_ref[:, g_lo:g_lo + gmlp_dim]))
    zq = _dot(hb, w_ref[:, 0:attn_dim])
    zkv = _dot(hb, w_ref[:, kv_lo:u_lo])

    vn = []
    for g in range(N_GROUPS):
        cols = slice(g * GROUP_DIM, (g + 1) * GROUP_DIM)
        t = gate[:, cols]
        dev = t - jnp.mean(t, axis=-1, keepdims=True)
        var = jnp.mean(dev * dev, axis=-1, keepdims=True)
        vn.append(((dev * lax.rsqrt(var + EPS)) * gng_ref[:, cols]).astype(BF16))

    qn = (zq * lax.rsqrt(head_mean_sq(zq, s_q_ref) + EPS)) * qg_ref[...]
    zk, zv = zkv[:, 0:KV_DIM], zkv[:, KV_DIM:2 * KV_DIM]
    kn = (zk * lax.rsqrt(head_mean_sq(zk, s_k_ref) + EPS)) * kg_ref[...]
    u = _gelu_tanh(_dot(hb, w_ref[:, u_lo:g_lo]))

    for j in range(attn_dim // LANES):
        cols = slice(j * LANES, (j + 1) * LANES)
        q_ref[:, cols] = (rope(qn[:, cols]) * (Q_SCALE * LOG2_E)).astype(BF16)
    k_ref[...] = rope(kn).astype(BF16)
    vt = zv.T
    ones = jnp.ones((SUM_ROWS, KEY_TILE), BF16)
    for c in range(tm // KEY_TILE):
        keys = slice(c * KEY_TILE, (c + 1) * KEY_TILE)
        for kvh in range(N_KV_HEADS):
            vt_ref[c, kvh * V_ROWS:kvh * V_ROWS + HEAD_DIM, :] = vt[kvh * HEAD_DIM:(kvh + 1) * HEAD_DIM, keys].astype(BF16)
            vt_ref[c, kvh * V_ROWS + HEAD_DIM:(kvh + 1) * V_ROWS, :] = ones

    for g in range(N_GROUPS):
        cols = slice(g * GROUP_DIM, (g + 1) * GROUP_DIM)
        for c in range(tm // GROUP_DIM):
            rows = slice(c * GROUP_DIM, (c + 1) * GROUP_DIM)
            mixed = _dot(ws_ref[g], vn[g][rows, :]) + bs_ref[:, g:g + 1]
            gm_ref[rows, cols] = (u[rows, cols] * mixed).astype(BF16)


def _inproj(layer, x_srcs, mods, g1, w_in, qg, kg, cos_t, sin_t, s_q, s_k, gng, ws, bs, cast_ws, *,
            lat_tiles, tiles_per_batch):
    split_input = len(x_srcs) == 2
    rows = sum(a.shape[0] for a in x_srcs)
    d = x_srcs[0].shape[1]
    in_dim = w_in.shape[-1]
    attn_dim = qg.shape[-1]
    gmlp_dim = gng.shape[-1]
    n_tiles = rows // ROW_TILE
    ctx_mod_row = lat_tiles // tiles_per_batch
    ident_tile = cos_t.shape[0] // ROW_TILE - 1

    def mod_idx(j):
        return (layer, jnp.where(j < lat_tiles, j // tiles_per_batch, ctx_mod_row), 0, 0)

    def rope_idx(j):
        return (jnp.where(j < lat_tiles, j % tiles_per_batch, ident_tile), 0)

    const2 = lambda j: (0, 0)
    row_spec = pl.BlockSpec((ROW_TILE, d), lambda j: (j, 0))
    if split_input:
        x_specs = [pl.BlockSpec((ROW_TILE, d), lambda j: (jnp.minimum(j, lat_tiles - 1), 0)),
                   pl.BlockSpec((ROW_TILE, d), lambda j: (jnp.maximum(j - lat_tiles, 0), 0))]
    else:
        x_specs = [row_spec]

    def cast_specs(w, w_layer):
        blk = (w.shape[1] // lat_tiles, w.shape[2])
        return (pl.BlockSpec((None,) + blk, lambda j: (w_layer, jnp.minimum(j, lat_tiles - 1), 0)),
                pl.BlockSpec(blk, lambda j: (jnp.minimum(j, lat_tiles - 1), 0)),
                jax.ShapeDtypeStruct(w.shape[1:], BF16))

    casts = [cast_specs(w, w_layer) for w, w_layer in cast_ws]
    kern = functools.partial(_inproj_kernel, d_model=d, attn_dim=attn_dim, gmlp_dim=gmlp_dim,
                             lat_tiles=lat_tiles, split_input=split_input, n_casts=len(casts))
    return pl.pallas_call(
        kern,
        grid=(n_tiles,),
        in_specs=x_specs + [
            pl.BlockSpec((None, None, 1, mods.shape[-1]), mod_idx),
            pl.BlockSpec((1, d), const2),
            pl.BlockSpec((d, in_dim), const2),
            pl.BlockSpec((1, attn_dim), const2),
            pl.BlockSpec((1, KV_DIM), const2),
            pl.BlockSpec((ROW_TILE, LANES), rope_idx),
            pl.BlockSpec((ROW_TILE, LANES), rope_idx),
            pl.BlockSpec(s_q.shape, const2),
            pl.BlockSpec(s_k.shape, const2),
            pl.BlockSpec((1, gmlp_dim), const2),
            pl.BlockSpec(ws.shape, lambda j: (0, 0, 0)),
            pl.BlockSpec(bs.shape, const2),
        ] + [c[0] for c in casts],
        out_specs=[
            pl.BlockSpec((ROW_TILE, attn_dim), lambda j: (j, 0)),
            pl.BlockSpec((ROW_TILE, KV_DIM), lambda j: (j, 0)),
            pl.BlockSpec((ROW_TILE // KEY_TILE, N_KV_HEADS * V_ROWS, KEY_TILE), lambda j: (j, 0, 0)),
            pl.BlockSpec((ROW_TILE, gmlp_dim), lambda j: (j, 0)),
        ] + [c[1] for c in casts] + ([row_spec] if split_input else []),
        out_shape=[
            jax.ShapeDtypeStruct((rows, attn_dim), BF16),
            jax.ShapeDtypeStruct((rows, KV_DIM), BF16),
            jax.ShapeDtypeStruct((rows // KEY_TILE, N_KV_HEADS * V_ROWS, KEY_TILE), BF16),
            jax.ShapeDtypeStruct((rows, gmlp_dim), BF16),
        ] + [c[2] for c in casts] + ([jax.ShapeDtypeStruct((rows, d), F32)] if split_input else []),
        compiler_params=_params("arbitrary"),
        name="inproj",
    )(*x_srcs, mods, g1, w_in, qg, kg, cos_t, sin_t, s_q, s_k, gng, ws, bs, *[w for w, _ in cast_ws])


def _attn_kernel(ql_ref, qc_ref, kl_ref, kc_ref, vtl_ref, vtc_ref, ol_ref, oc_ref, s_ref, *, n_lat_steps):
    tq = Q_SUB
    n_cols = Q_PER_KV * tq
    group_w = Q_PER_KV * HEAD_DIM
    heads = range(N_KV_HEADS)
    refs = {"ctx": (kc_ref, vtc_ref), "lat": (kl_ref, vtl_ref)}

    def key_slabs(kinds):
        return [(kind, j) for kind in kinds for j in range(refs[kind][1].shape[0])]

    def stacked_qt(q_ref, sub, kvh):
        rows = slice(sub * tq, (sub + 1) * tq)
        qt = q_ref[rows, kvh * group_w:(kvh + 1) * group_w].astype(F32).T
        qst = jnp.concatenate([qt[h * HEAD_DIM:(h + 1) * HEAD_DIM, :] for h in range(Q_PER_KV)], axis=1)
        parts = [jnp.zeros_like(qst)] * N_KV_HEADS
        parts[kvh] = qst
        return jnp.concatenate(parts, axis=0).astype(BF16)

    def put_scores(slot, slab, kvh, qst):
        kind, j = slab
        s = _dot(refs[kind][0][j * KEY_TILE:(j + 1) * KEY_TILE, :], qst)
        s_ref[slot, kvh] = s
        return jnp.max(s, axis=0, keepdims=True)

    def update(slot, slab, kvh, s_max, carry):
        kind, j = slab
        m, acc = carry
        m_new = jnp.maximum(m, s_max)
        alpha = jnp.exp2(m - m_new)
        p = jnp.exp2(s_ref[slot, kvh] - m_new)
        vt = refs[kind][1][j, kvh * V_ROWS:(kvh + 1) * V_ROWS, :]
        return m_new, alpha * acc + _dot(vt, p.astype(BF16))

    def finish(o_ref, sub, kvh, carry):
        acc = carry[1]
        ot = acc[0:HEAD_DIM, :] / acc[HEAD_DIM:HEAD_DIM + 1, :]
        ot = jnp.concatenate([ot[:, h * tq:(h + 1) * tq] for h in range(Q_PER_KV)], axis=0)
        o_ref[sub * tq:(sub + 1) * tq, kvh * group_w:(kvh + 1) * group_w] = ot.T.astype(BF16)

    def attend(q_ref, o_ref, slabs):
        n_sub = q_ref.shape[0] // tq
        items = [(sub, slab, kvh) for sub in range(n_sub) for slab in slabs for kvh in heads]
        qst, s_max, carry = {}, {}, {}

        assert SCORE_LOOKAHEAD % N_KV_HEADS == 0

        def slot(u):
            return (u // N_KV_HEADS) % SCORE_SLOTS

        def put(u):
            sub, slab, kvh = items[u]
            if (sub, kvh) not in qst:
                qst[sub, kvh] = stacked_qt(q_ref, sub, kvh)
            s_max[u] = put_scores(slot(u), slab, kvh, qst[sub, kvh])

        for u in range(min(SCORE_LOOKAHEAD, len(items))):
            put(u)
        for u, (sub, slab, kvh) in enumerate(items):
            if u + SCORE_LOOKAHEAD < len(items):
                put(u + SCORE_LOOKAHEAD)
            if slab is slabs[0]:
                carry[kvh] = (jnp.full((1, n_cols), -jnp.inf, F32), jnp.zeros((V_ROWS, n_cols), F32))
            carry[kvh] = update(slot(u), slab, kvh, s_max.pop(u), carry[kvh])
            if slab is slabs[-1]:
                finish(o_ref, sub, kvh, carry[kvh])

    i = pl.program_id(1)
    pl.when(i < n_lat_steps)(lambda: attend(ql_ref, ol_ref, key_slabs(("ctx", "lat"))))
    pl.when(i >= n_lat_steps)(lambda: attend(qc_ref, oc_ref, key_slabs(("ctx",))))


def _attention(q, k, vt, *, batch, seq, ctx_len):
    rows, attn_dim = q.shape
    lat_rows = batch * seq
    n_lat_steps = seq // Q_TILE
    vt_rows = vt.shape[1]
    assert seq % KEY_TILE == 0 and ctx_len % KEY_TILE == 0 and ctx_len % Q_SUB == 0, (seq, ctx_len)

    def lat_idx(b, i):
        return (b * n_lat_steps + jnp.minimum(i, n_lat_steps - 1), 0)

    return pl.pallas_call(
        functools.partial(_attn_kernel, n_lat_steps=n_lat_steps),
        grid=(batch, n_lat_steps + 1),
        in_specs=[
            pl.BlockSpec((Q_TILE, attn_dim), lat_idx),
            pl.BlockSpec((ctx_len, attn_dim), lambda b, i: (lat_rows // ctx_len + b, 0)),
            pl.BlockSpec((seq, KV_DIM), lambda b, i: (b, 0)),
            pl.BlockSpec((ctx_len, KV_DIM), lambda b, i: (lat_rows // ctx_len + b, 0)),
            pl.BlockSpec((seq // KEY_TILE, vt_rows, KEY_TILE), lambda b, i: (b, 0, 0)),
            pl.BlockSpec((ctx_len // KEY_TILE, vt_rows, KEY_TILE), lambda b, i: (lat_rows // ctx_len + b, 0, 0)),
        ],
        out_specs=[
            pl.BlockSpec((Q_TILE, attn_dim), lat_idx),
            pl.BlockSpec((ctx_len, attn_dim), lambda b, i: (b, 0)),
        ],
        out_shape=[
            jax.ShapeDtypeStruct((lat_rows, attn_dim), BF16),
            jax.ShapeDtypeStruct((rows - lat_rows, attn_dim), BF16),
        ],
        scratch_shapes=[pltpu.VMEM((SCORE_SLOTS, N_KV_HEADS, KEY_TILE, Q_PER_KV * Q_SUB), F32)],
        compiler_params=_params("arbitrary", "arbitrary"),
        name="attention",
    )(q, q, k, k, vt, vt)


def _out_ffn_kernel(x_ref, atl_ref, atc_ref, gm_ref, mod_ref, g2_ref, wo_ref, w1_ref, w2_ref, o_ref, *, d_model,
                    lat_tiles):
    mod = mod_ref[...]
    gate1 = mod[:, 2 * d_model:3 * d_model]
    shift = mod[:, 3 * d_model:4 * d_model]
    scale = mod[:, 4 * d_model:5 * d_model]
    gate2 = mod[:, 5 * d_model:6 * d_model]
    attn_dim = atl_ref.shape[1]
    at = jnp.where(pl.program_id(0) < lat_tiles, atl_ref[...], atc_ref[...])
    y = _dot(at, wo_ref[0:attn_dim, :]) + _dot(gm_ref[...], wo_ref[attn_dim:, :])
    x1 = x_ref[...] + gate1 * y
    ms = jnp.mean(x1 * x1, axis=-1, keepdims=True)
    h = (x1 * lax.rsqrt(ms + EPS)) * g2_ref[...]
    hb = (h * (1.0 + scale) + shift).astype(BF16)
    acc = None
    for c in range(w1_ref.shape[1] // FF_TILE):
        cols = slice(c * FF_TILE, (c + 1) * FF_TILE)
        t = jnp.maximum(_dot(hb, w1_ref[:, cols]), 0.0)
        part = _dot((t * t).astype(BF16), w2_ref[cols, :])
        acc = part if acc is None else acc + part
    o_ref[...] = x1 + gate2 * acc


def _out_ffn(layer, xs, attn_lat, attn_ctx, gm, mods, g2, w_out, w_ff1, w_ff2, *, n_tiles, lat_tiles,
             tiles_per_batch):
    d = xs.shape[1]
    attn_dim = attn_lat.shape[1]
    ff = w_ff1.shape[-1]
    ctx_mod_row = lat_tiles // tiles_per_batch

    def mod_idx(j):
        return (layer, jnp.where(j < lat_tiles, j // tiles_per_batch, ctx_mod_row), 0, 0)

    resident = pl.Buffered(1)
    kern = functools.partial(_out_ffn_kernel, d_model=d, lat_tiles=lat_tiles)
    return pl.pallas_call(
        kern,
        grid=(n_tiles,),
        in_specs=[
            pl.BlockSpec((ROW_TILE, d), lambda j: (j, 0)),
            pl.BlockSpec((ROW_TILE, attn_dim), lambda j: (jnp.minimum(j, lat_tiles - 1), 0)),
            pl.BlockSpec((ROW_TILE, attn_dim), lambda j: (jnp.maximum(j - lat_tiles, 0), 0)),
            pl.BlockSpec((ROW_TILE, gm.shape[1]), lambda j: (j, 0)),
            pl.BlockSpec((None, None, 1, mods.shape[-1]), mod_idx),
            pl.BlockSpec((1, d), lambda j: (0, 0)),
            pl.BlockSpec(w_out.shape, lambda j: (0, 0), pipeline_mode=resident),
            pl.BlockSpec((d, ff), lambda j: (0, 0), pipeline_mode=resident),
            pl.BlockSpec((ff, d), lambda j: (0, 0), pipeline_mode=resident),
        ],
        out_specs=pl.BlockSpec((ROW_TILE, d), lambda j: (j, 0)),
        out_shape=jax.ShapeDtypeStruct((n_tiles * ROW_TILE, d), F32),
        compiler_params=_params("arbitrary"),
        name="out_ffn",
    )(xs, attn_lat, attn_ctx, gm, mods, g2, w_out, w_ff1, w_ff2)


def _rope_tables(seq):
    n_rows = seq // GRID_W
    n_freq = HEAD_DIM // 4
    inv_freq = ROPE_THETA ** (-jnp.arange(n_freq, dtype=F32) / n_freq)
    ang_row = jnp.arange(n_rows, dtype=F32)[:, None] * inv_freq
    ang_col = jnp.arange(GRID_W, dtype=F32)[:, None] * inv_freq

    def table(fn, sign):
        by_row = jnp.repeat(fn(ang_row), GRID_W, axis=0)
        by_col = jnp.tile(fn(ang_col), (n_rows, 1))
        head = jnp.concatenate([sign[0] * by_row, sign[1] * by_row, sign[0] * by_col, sign[1] * by_col], axis=1)
        return jnp.tile(head, (1, LANES // HEAD_DIM))

    cos_t = jnp.concatenate([table(jnp.cos, (1.0, 1.0)), jnp.ones((ROW_TILE, LANES), F32)], axis=0)
    sin_t = jnp.concatenate([table(jnp.sin, (-1.0, 1.0)), jnp.zeros((ROW_TILE, LANES), F32)], axis=0)
    return cos_t, sin_t


def _head_indicator(width):
    idx = np.arange(width) // HEAD_DIM
    return jnp.asarray((idx[:, None] == idx[None, :]).astype(np.float32), dtype=BF16)


def kernel(x, c, ctx, c_ctx, w_mod, b_mod, norm1_g, w_in, q_norm_g, k_norm_g, gmlp_norm_g, w_spatial, b_spatial,
           w_out, norm2_g, w_ff1, w_ff2):
    batch, seq, d = x.shape
    ctx_len = ctx.shape[1]
    depth = w_mod.shape[0]
    gmlp_dim = gmlp_norm_g.shape[-1]
    attn_dim = w_out.shape[1] - gmlp_dim
    lat_rows = batch * seq
    lat_tiles = lat_rows // ROW_TILE
    tiles_per_batch = seq // ROW_TILE

    cond = jnp.concatenate([c, c_ctx[None, :], jnp.zeros((MOD_ROWS - batch - 1, d), F32)], axis=0)
    mods = _modulation(cond, w_mod, b_mod).reshape(depth, MOD_ROWS, 1, N_MOD * d)

    cos_t, sin_t = _rope_tables(seq)
    s_q = _head_indicator(attn_dim)
    s_k = _head_indicator(KV_DIM)
    w_in_b = w_in[0].astype(BF16)
    w_sp_b = w_spatial.astype(BF16)

    x_srcs = (x.reshape(lat_rows, d), ctx.reshape(batch * ctx_len, d))
    for l in range(depth):
        last = l == depth - 1
        cast_ws = [(w_out, l), (w_ff1, l), (w_ff2, l)] + ([] if last else [(w_in, l + 1)])
        outs = _inproj(
            l, x_srcs, mods, norm1_g[l][None, :], w_in_b,
            jnp.tile(q_norm_g[l], attn_dim // HEAD_DIM)[None, :],
            jnp.tile(k_norm_g[l], KV_DIM // HEAD_DIM)[None, :],
            cos_t, sin_t, s_q, s_k, gmlp_norm_g[l][None, :], w_sp_b[l], b_spatial[l].T, cast_ws,
            lat_tiles=lat_tiles, tiles_per_batch=tiles_per_batch)
        q, k, vt, gm, w_out_b, w_ff1_b, w_ff2_b = outs[:7]
        outs = outs[7:]
        if not last:
            w_in_b, outs = outs[0], outs[1:]
        xs = outs[0] if len(x_srcs) == 2 else x_srcs[0]
        attn_lat, attn_ctx = _attention(q, k, vt, batch=batch, seq=seq, ctx_len=ctx_len)
        n_tiles = lat_tiles if last else xs.shape[0] // ROW_TILE
        xs = _out_ffn(l, xs, attn_lat, attn_ctx, gm, mods, norm2_g[l][None, :], w_out_b, w_ff1_b, w_ff2_b,
                      n_tiles=n_tiles, lat_tiles=lat_tiles, tiles_per_batch=tiles_per_batch)
        x_srcs = (xs,)
    return xs.reshape(batch, seq, d)
```

```python
import functools

import numpy as np
import jax
import jax.numpy as jnp
from jax import lax
from jax.experimental import pallas as pl
from jax.experimental.pallas import tpu as pltpu

F32 = jnp.float32
BF16 = jnp.bfloat16

HEAD_DIM = 64
Q_PER_KV = 4
N_KV_HEADS = 2
KV_DIM = N_KV_HEADS * HEAD_DIM
GROUP_DIM = 128
N_GROUPS = 4
GRID_W = 64
ROPE_THETA = 10000.0
EPS = 1e-6
N_MOD = 6
Q_SCALE = HEAD_DIM ** -0.5
LOG2_E = 1.4426950408889634

LANES = 128
ROW_TILE = 512
INPROJ_TILE = 2 * ROW_TILE
KEY_TILE = 256
SCORE_LOOKAHEAD = 2
SCORE_SLOTS = SCORE_LOOKAHEAD // 2 + 1
SUM_ROWS = 16
V_ROWS = HEAD_DIM + SUM_ROWS
Q_SUB = 128
Q_TILE = 512
FF_TILE = 1024
MOD_COL_TILE = 2048
MOD_ROWS = 8
VMEM_LIMIT = 56 * 1024 * 1024


def _dot(a, b):
    return jnp.dot(a, b, preferred_element_type=F32)


def _gelu_tanh(x):
    c = np.sqrt(2.0 / np.pi).astype(np.float32)
    inner = x * ((x * x) * (c * np.float32(0.044715)) + c)
    half = 0.5 * x
    return half + half * jnp.tanh(inner)


def _split_bf16(t):
    hi = t.astype(BF16)
    lo = (t - hi.astype(F32)).astype(BF16)
    return hi, lo


def _params(*sem):
    return pltpu.CompilerParams(dimension_semantics=sem, vmem_limit_bytes=VMEM_LIMIT)


def _mod_kernel(cond_ref, w_ref, b_ref, o_ref):
    cnd = cond_ref[...]
    act = cnd * (1.0 / (1.0 + jnp.exp(-cnd)))
    a_hi, a_lo = _split_bf16(act)
    w_hi, w_lo = _split_bf16(w_ref[...])
    acc = _dot(a_hi, w_hi) + _dot(a_lo, w_hi) + _dot(a_hi, w_lo)
    o_ref[...] = acc + b_ref[...]


def _modulation(cond, w_mod, b_mod):
    depth, d, n_out = w_mod.shape
    return pl.pallas_call(
        _mod_kernel,
        grid=(depth, n_out // MOD_COL_TILE),
        in_specs=[
            pl.BlockSpec((MOD_ROWS, d), lambda l, j: (0, 0)),
            pl.BlockSpec((None, d, MOD_COL_TILE), lambda l, j: (l, 0, j)),
            pl.BlockSpec((None, 1, MOD_COL_TILE), lambda l, j: (l, 0, j)),
        ],
        out_specs=pl.BlockSpec((None, MOD_ROWS, MOD_COL_TILE), lambda l, j: (l, 0, j)),
        out_shape=jax.ShapeDtypeStruct((depth, MOD_ROWS, n_out), F32),
        compiler_params=_params("arbitrary", "arbitrary"),
        name="modulation",
    )(cond, w_mod, b_mod.reshape(depth, 1, n_out))


def _inproj_kernel(*refs, d_model, attn_dim, gmlp_dim, lat_tiles, split_input, n_casts):
    n_x = 2 if split_input else 1
    x_refs, refs = refs[:n_x], refs[n_x:]
    (mod_ref, g1_ref, w_ref, qg_ref, kg_ref, cos_ref, sin_ref, s_q_ref, s_k_ref, gng_ref, ws_ref, bs_ref,
     *refs) = refs
    cast_src, (q_ref, k_ref, vt_ref, gm_ref, *refs) = refs[:n_casts], refs[n_casts:]
    cast_dst, xs_refs = refs[:n_casts], refs[n_casts:]
    if split_input:
        x = jnp.where(pl.program_id(0) < lat_tiles, x_refs[0][...], x_refs[1][...])
        xs_refs[0][...] = x
    else:
        x = x_refs[0][...]
    tm = ROW_TILE
    for src, dst in zip(cast_src, cast_dst):
        dst[...] = src[...].astype(BF16)
    mod = mod_ref[...]
    shift, scale = mod[:, 0:d_model], mod[:, d_model:2 * d_model]

    def normed(xh):
        ms = jnp.mean(xh * xh, axis=-1, keepdims=True)
        h = (xh * lax.rsqrt(ms + EPS)) * g1_ref[...]
        return (h * (1.0 + scale) + shift).astype(BF16)

    parts = [slice(r, r + tm) for r in range(0, x.shape[0], tm)]
    hbs = [normed(x[rows, :]) for rows in parts]
    for part, hb in zip(parts, hbs):
        _inproj_part(part, hb, w_ref, qg_ref, kg_ref, cos_ref, sin_ref, s_q_ref, s_k_ref, gng_ref, ws_ref, bs_ref,
                     q_ref, k_ref, vt_ref, gm_ref, attn_dim=attn_dim, gmlp_dim=gmlp_dim)


def _inproj_part(part, hb, w_ref, qg_ref, kg_ref, cos_ref, sin_ref, s_q_ref, s_k_ref, gng_ref, ws_ref, bs_ref,
                 q_ref, k_ref, vt_ref, gm_ref, *, attn_dim, gmlp_dim):
    tm = hb.shape[0]
    cos, sin = cos_ref[part, :], sin_ref[part, :]
    lane = lax.broadcasted_iota(jnp.int32, (tm, LANES), 1)
    second_half = (lane & (HEAD_DIM // 4)) != 0

    def rope(t):
        partner = jnp.where(second_half, pltpu.roll(t, HEAD_DIM // 4, 1),
                            pltpu.roll(t, LANES - HEAD_DIM // 4, 1))
        return t * cos + partner * sin

    def head_mean_sq(t, s_ref):
        return _dot((t * t).astype(BF16), s_ref[...]) * (1.0 / HEAD_DIM)

    kv_lo = attn_dim
    u_lo = attn_dim + 2 * KV_DIM
    g_lo = u_lo + gmlp_dim

    gate = _gelu_tanh(_dot(hb, w_ref[:, g_lo:g_lo + gmlp_dim]))
    zq = _dot(hb, w_ref[:, 0:attn_dim])
    zkv = _dot(hb, w_ref[:, kv_lo:u_lo])

    vn = []
    for g in range(N_GROUPS):
        cols = slice(g * GROUP_DIM, (g + 1) * GROUP_DIM)
        t = gate[:, cols]
        dev = t - jnp.mean(t, axis=-1, keepdims=True)
        var = jnp.mean(dev * dev, axis=-1, keepdims=True)
        vn.append(((dev * lax.rsqrt(var + EPS)) * gng_ref[:, cols]).astype(BF16))

    qn = (zq * lax.rsqrt(head_mean_sq(zq, s_q_ref) + EPS)) * qg_ref[...]
    zk, zv = zkv[:, 0:KV_DIM], zkv[:, KV_DIM:2 * KV_DIM]
    kn = (zk * lax.rsqrt(head_mean_sq(zk, s_k_ref) + EPS)) * kg_ref[...]
    u = _gelu_tanh(_dot(hb, w_ref[:, u_lo:g_lo]))

    for j in range(attn_dim // LANES):
        cols = slice(j * LANES, (j + 1) * LANES)
        q_ref[part, cols] = (rope(qn[:, cols]) * (Q_SCALE * LOG2_E)).astype(BF16)
    k_ref[part, :] = rope(kn).astype(BF16)
    vt = zv.T
    ones = jnp.ones((SUM_ROWS, KEY_TILE), BF16)
    for c in range(tm // KEY_TILE):
        keys = slice(c * KEY_TILE, (c + 1) * KEY_TILE)
        slab = part.start // KEY_TILE + c
        for kvh in range(N_KV_HEADS):
            vt_ref[slab, kvh * V_ROWS:kvh * V_ROWS + HEAD_DIM, :] = vt[kvh * HEAD_DIM:(kvh + 1) * HEAD_DIM, keys].astype(BF16)
            vt_ref[slab, kvh * V_ROWS + HEAD_DIM:(kvh + 1) * V_ROWS, :] = ones

    for g in range(N_GROUPS):
        cols = slice(g * GROUP_DIM, (g + 1) * GROUP_DIM)
        for c in range(tm // GROUP_DIM):
            rows = slice(c * GROUP_DIM, (c + 1) * GROUP_DIM)
            mixed = _dot(ws_ref[g], vn[g][rows, :]) + bs_ref[:, g:g + 1]
            out_rows = slice(part.start + rows.start, part.start + rows.stop)
            gm_ref[out_rows, cols] = (u[rows, cols] * mixed).astype(BF16)


def _inproj(layer, x_srcs, mods, g1, w_in, qg, kg, cos_t, sin_t, s_q, s_k, gng, ws, bs, cast_ws, *,
            lat_tiles, tiles_per_batch):
    split_input = len(x_srcs) == 2
    rows = sum(a.shape[0] for a in x_srcs)
    d = x_srcs[0].shape[1]
    in_dim = w_in.shape[-1]
    attn_dim = qg.shape[-1]
    gmlp_dim = gng.shape[-1]
    tile = INPROJ_TILE
    n_tiles = rows // tile
    ctx_mod_row = lat_tiles // tiles_per_batch
    ident_tile = cos_t.shape[0] // tile - 1

    def mod_idx(j):
        return (layer, jnp.where(j < lat_tiles, j // tiles_per_batch, ctx_mod_row), 0, 0)

    def rope_idx(j):
        return (jnp.where(j < lat_tiles, j % tiles_per_batch, ident_tile), 0)

    const2 = lambda j: (0, 0)
    row_spec = pl.BlockSpec((tile, d), lambda j: (j, 0))
    if split_input:
        x_specs = [pl.BlockSpec((tile, d), lambda j: (jnp.minimum(j, lat_tiles - 1), 0)),
                   pl.BlockSpec((tile, d), lambda j: (jnp.maximum(j - lat_tiles, 0), 0))]
    else:
        x_specs = [row_spec]

    def cast_specs(w, w_layer):
        blk = (w.shape[1] // lat_tiles, w.shape[2])
        return (pl.BlockSpec((None,) + blk, lambda j: (w_layer, jnp.minimum(j, lat_tiles - 1), 0)),
                pl.BlockSpec(blk, lambda j: (jnp.minimum(j, lat_tiles - 1), 0)),
                jax.ShapeDtypeStruct(w.shape[1:], BF16))

    casts = [cast_specs(w, w_layer) for w, w_layer in cast_ws]
    kern = functools.partial(_inproj_kernel, d_model=d, attn_dim=attn_dim, gmlp_dim=gmlp_dim,
                             lat_tiles=lat_tiles, split_input=split_input, n_casts=len(casts))
    return pl.pallas_call(
        kern,
        grid=(n_tiles,),
        in_specs=x_specs + [
            pl.BlockSpec((None, None, 1, mods.shape[-1]), mod_idx),
            pl.BlockSpec((1, d), const2),
            pl.BlockSpec((d, in_dim), const2),
            pl.BlockSpec((1, attn_dim), const2),
            pl.BlockSpec((1, KV_DIM), const2),
            pl.BlockSpec((tile, LANES), rope_idx),
            pl.BlockSpec((tile, LANES), rope_idx),
            pl.BlockSpec(s_q.shape, const2),
            pl.BlockSpec(s_k.shape, const2),
            pl.BlockSpec((1, gmlp_dim), const2),
            pl.BlockSpec(ws.shape, lambda j: (0, 0, 0)),
            pl.BlockSpec(bs.shape, const2),
        ] + [c[0] for c in casts],
        out_specs=[
            pl.BlockSpec((tile, attn_dim), lambda j: (j, 0)),
            pl.BlockSpec((tile, KV_DIM), lambda j: (j, 0)),
            pl.BlockSpec((tile // KEY_TILE, N_KV_HEADS * V_ROWS, KEY_TILE), lambda j: (j, 0, 0)),
            pl.BlockSpec((tile, gmlp_dim), lambda j: (j, 0)),
        ] + [c[1] for c in casts] + ([row_spec] if split_input else []),
        out_shape=[
            jax.ShapeDtypeStruct((rows, attn_dim), BF16),
            jax.ShapeDtypeStruct((rows, KV_DIM), BF16),
            jax.ShapeDtypeStruct((rows // KEY_TILE, N_KV_HEADS * V_ROWS, KEY_TILE), BF16),
            jax.ShapeDtypeStruct((rows, gmlp_dim), BF16),
        ] + [c[2] for c in casts] + ([jax.ShapeDtypeStruct((rows, d), F32)] if split_input else []),
        compiler_params=_params("arbitrary"),
        name="inproj",
    )(*x_srcs, mods, g1, w_in, qg, kg, cos_t, sin_t, s_q, s_k, gng, ws, bs, *[w for w, _ in cast_ws])


def _attn_kernel(ql_ref, qc_ref, kl_ref, kc_ref, vtl_ref, vtc_ref, ol_ref, oc_ref, s_ref, *, n_lat_steps):
    tq = Q_SUB
    n_cols = Q_PER_KV * tq
    group_w = Q_PER_KV * HEAD_DIM
    heads = range(N_KV_HEADS)
    refs = {"ctx": (kc_ref, vtc_ref), "lat": (kl_ref, vtl_ref)}

    def key_slabs(kinds):
        return [(kind, j) for kind in kinds for j in range(refs[kind][1].shape[0])]

    def stacked_qt(q_ref, sub, kvh):
        rows = slice(sub * tq, (sub + 1) * tq)
        qt = q_ref[rows, kvh * group_w:(kvh + 1) * group_w].astype(F32).T
        qst = jnp.concatenate([qt[h * HEAD_DIM:(h + 1) * HEAD_DIM, :] for h in range(Q_PER_KV)], axis=1)
        parts = [jnp.zeros_like(qst)] * N_KV_HEADS
        parts[kvh] = qst
        return jnp.concatenate(parts, axis=0).astype(BF16)

    def put_scores(slot, slab, kvh, qst):
        kind, j = slab
        s = _dot(refs[kind][0][j * KEY_TILE:(j + 1) * KEY_TILE, :], qst)
        s_ref[slot, kvh] = s
        return jnp.max(s, axis=0, keepdims=True)

    def update(slot, slab, kvh, s_max, carry):
        kind, j = slab
        m, acc = carry
        m_new = jnp.maximum(m, s_max)
        alpha = jnp.exp2(m - m_new)
        p = jnp.exp2(s_ref[slot, kvh] - m_new)
        vt = refs[kind][1][j, kvh * V_ROWS:(kvh + 1) * V_ROWS, :]
        return m_new, alpha * acc + _dot(vt, p.astype(BF16))

    def finish(o_ref, sub, kvh, carry):
        acc = carry[1]
        ot = acc[0:HEAD_DIM, :] / acc[HEAD_DIM:HEAD_DIM + 1, :]
        ot = jnp.concatenate([ot[:, h * tq:(h + 1) * tq] for h in range(Q_PER_KV)], axis=0)
        o_ref[sub * tq:(sub + 1) * tq, kvh * group_w:(kvh + 1) * group_w] = ot.T.astype(BF16)

    def attend(q_ref, o_ref, slabs):
        n_sub = q_ref.shape[0] // tq
        items = [(sub, slab, kvh) for sub in range(n_sub) for slab in slabs for kvh in heads]
        qst, s_max, carry = {}, {}, {}

        assert SCORE_LOOKAHEAD % N_KV_HEADS == 0

        def slot(u):
            return (u // N_KV_HEADS) % SCORE_SLOTS

        def put(u):
            sub, slab, kvh = items[u]
            if (sub, kvh) not in qst:
                qst[sub, kvh] = stacked_qt(q_ref, sub, kvh)
            s_max[u] = put_scores(slot(u), slab, kvh, qst[sub, kvh])

        for u in range(min(SCORE_LOOKAHEAD, len(items))):
            put(u)
        for u, (sub, slab, kvh) in enumerate(items):
            if u + SCORE_LOOKAHEAD < len(items):
                put(u + SCORE_LOOKAHEAD)
            if slab is slabs[0]:
                carry[kvh] = (jnp.full((1, n_cols), -jnp.inf, F32), jnp.zeros((V_ROWS, n_cols), F32))
            carry[kvh] = update(slot(u), slab, kvh, s_max.pop(u), carry[kvh])
            if slab is slabs[-1]:
                finish(o_ref, sub, kvh, carry[kvh])

    i = pl.program_id(1)
    pl.when(i < n_lat_steps)(lambda: attend(ql_ref, ol_ref, key_slabs(("ctx", "lat"))))
    pl.when(i >= n_lat_steps)(lambda: attend(qc_ref, oc_ref, key_slabs(("ctx",))))


def _attention(q, k, vt, *, batch, seq, ctx_len):
    rows, attn_dim = q.shape
    lat_rows = batch * seq
    n_lat_steps = seq // Q_TILE
    vt_rows = vt.shape[1]
    assert seq % KEY_TILE == 0 and ctx_len % KEY_TILE == 0 and ctx_len % Q_SUB == 0, (seq, ctx_len)

    def lat_idx(b, i):
        return (b * n_lat_steps + jnp.minimum(i, n_lat_steps - 1), 0)

    return pl.pallas_call(
        functools.partial(_attn_kernel, n_lat_steps=n_lat_steps),
        grid=(batch, n_lat_steps + 1),
        in_specs=[
            pl.BlockSpec((Q_TILE, attn_dim), lat_idx),
            pl.BlockSpec((ctx_len, attn_dim), lambda b, i: (lat_rows // ctx_len + b, 0)),
            pl.BlockSpec((seq, KV_DIM), lambda b, i: (b, 0)),
            pl.BlockSpec((ctx_len, KV_DIM), lambda b, i: (lat_rows // ctx_len + b, 0)),
            pl.BlockSpec((seq // KEY_TILE, vt_rows, KEY_TILE), lambda b, i: (b, 0, 0)),
            pl.BlockSpec((ctx_len // KEY_TILE, vt_rows, KEY_TILE), lambda b, i: (lat_rows // ctx_len + b, 0, 0)),
        ],
        out_specs=[
            pl.BlockSpec((Q_TILE, attn_dim), lat_idx),
            pl.BlockSpec((ctx_len, attn_dim), lambda b, i: (b, 0)),
        ],
        out_shape=[
            jax.ShapeDtypeStruct((lat_rows, attn_dim), BF16),
            jax.ShapeDtypeStruct((rows - lat_rows, attn_dim), BF16),
        ],
        scratch_shapes=[pltpu.VMEM((SCORE_SLOTS, N_KV_HEADS, KEY_TILE, Q_PER_KV * Q_SUB), F32)],
        compiler_params=_params("arbitrary", "arbitrary"),
        name="attention",
    )(q, q, k, k, vt, vt)


def _out_ffn_kernel(x_ref, atl_ref, atc_ref, gm_ref, mod_ref, g2_ref, wo_ref, w1_ref, w2_ref, o_ref, *, d_model,
                    lat_tiles):
    mod = mod_ref[...]
    gate1 = mod[:, 2 * d_model:3 * d_model]
    shift = mod[:, 3 * d_model:4 * d_model]
    scale = mod[:, 4 * d_model:5 * d_model]
    gate2 = mod[:, 5 * d_model:6 * d_model]
    attn_dim = atl_ref.shape[1]
    at = jnp.where(pl.program_id(0) < lat_tiles, atl_ref[...], atc_ref[...])
    y = _dot(at, wo_ref[0:attn_dim, :]) + _dot(gm_ref[...], wo_ref[attn_dim:, :])
    x1 = x_ref[...] + gate1 * y
    ms = jnp.mean(x1 * x1, axis=-1, keepdims=True)
    h = (x1 * lax.rsqrt(ms + EPS)) * g2_ref[...]
    hb = (h * (1.0 + scale) + shift).astype(BF16)
    acc = None
    for c in range(w1_ref.shape[1] // FF_TILE):
        cols = slice(c * FF_TILE, (c + 1) * FF_TILE)
        t = jnp.maximum(_dot(hb, w1_ref[:, cols]), 0.0)
        part = _dot((t * t).astype(BF16), w2_ref[cols, :])
        acc = part if acc is None else acc + part
    o_ref[...] = x1 + gate2 * acc


def _out_ffn(layer, xs, attn_lat, attn_ctx, gm, mods, g2, w_out, w_ff1, w_ff2, *, n_tiles, lat_tiles,
             tiles_per_batch):
    d = xs.shape[1]
    attn_dim = attn_lat.shape[1]
    ff = w_ff1.shape[-1]
    ctx_mod_row = lat_tiles // tiles_per_batch

    def mod_idx(j):
        return (layer, jnp.where(j < lat_tiles, j // tiles_per_batch, ctx_mod_row), 0, 0)

    resident = pl.Buffered(1)
    kern = functools.partial(_out_ffn_kernel, d_model=d, lat_tiles=lat_tiles)
    return pl.pallas_call(
        kern,
        grid=(n_tiles,),
        in_specs=[
            pl.BlockSpec((ROW_TILE, d), lambda j: (j, 0)),
            pl.BlockSpec((ROW_TILE, attn_dim), lambda j: (jnp.minimum(j, lat_tiles - 1), 0)),
            pl.BlockSpec((ROW_TILE, attn_dim), lambda j: (jnp.maximum(j - lat_tiles, 0), 0)),
            pl.BlockSpec((ROW_TILE, gm.shape[1]), lambda j: (j, 0)),
            pl.BlockSpec((None, None, 1, mods.shape[-1]), mod_idx),
            pl.BlockSpec((1, d), lambda j: (0, 0)),
            pl.BlockSpec(w_out.shape, lambda j: (0, 0), pipeline_mode=resident),
            pl.BlockSpec((d, ff), lambda j: (0, 0), pipeline_mode=resident),
            pl.BlockSpec((ff, d), lambda j: (0, 0), pipeline_mode=resident),
        ],
        out_specs=pl.BlockSpec((ROW_TILE, d), lambda j: (j, 0)),
        out_shape=jax.ShapeDtypeStruct((n_tiles * ROW_TILE, d), F32),
        compiler_params=_params("arbitrary"),
        name="out_ffn",
    )(xs, attn_lat, attn_ctx, gm, mods, g2, w_out, w_ff1, w_ff2)


def _rope_tables(seq):
    n_rows = seq // GRID_W
    n_freq = HEAD_DIM // 4
    inv_freq = ROPE_THETA ** (-jnp.arange(n_freq, dtype=F32) / n_freq)
    ang_row = jnp.arange(n_rows, dtype=F32)[:, None] * inv_freq
    ang_col = jnp.arange(GRID_W, dtype=F32)[:, None] * inv_freq

    def table(fn, sign):
        by_row = jnp.repeat(fn(ang_row), GRID_W, axis=0)
        by_col = jnp.tile(fn(ang_col), (n_rows, 1))
        head = jnp.concatenate([sign[0] * by_row, sign[1] * by_row, sign[0] * by_col, sign[1] * by_col], axis=1)
        return jnp.tile(head, (1, LANES // HEAD_DIM))

    cos_t = jnp.concatenate([table(jnp.cos, (1.0, 1.0)), jnp.ones((INPROJ_TILE, LANES), F32)], axis=0)
    sin_t = jnp.concatenate([table(jnp.sin, (-1.0, 1.0)), jnp.zeros((INPROJ_TILE, LANES), F32)], axis=0)
    return cos_t, sin_t


def _head_indicator(width):
    idx = np.arange(width) // HEAD_DIM
    return jnp.asarray((idx[:, None] == idx[None, :]).astype(np.float32), dtype=BF16)


def kernel(x, c, ctx, c_ctx, w_mod, b_mod, norm1_g, w_in, q_norm_g, k_norm_g, gmlp_norm_g, w_spatial, b_spatial,
           w_out, norm2_g, w_ff1, w_ff2):
    batch, seq, d = x.shape
    ctx_len = ctx.shape[1]
    depth = w_mod.shape[0]
    gmlp_dim = gmlp_norm_g.shape[-1]
    attn_dim = w_out.shape[1] - gmlp_dim
    lat_rows = batch * seq
    lat_tiles = lat_rows // ROW_TILE
    tiles_per_batch = seq // ROW_TILE

    cond = jnp.concatenate([c, c_ctx[None, :], jnp.zeros((MOD_ROWS - batch - 1, d), F32)], axis=0)
    mods = _modulation(cond, w_mod, b_mod).reshape(depth, MOD_ROWS, 1, N_MOD * d)

    cos_t, sin_t = _rope_tables(seq)
    s_q = _head_indicator(attn_dim)
    s_k = _head_indicator(KV_DIM)
    w_in_b = w_in[0].astype(BF16)
    w_sp_b = w_spatial.astype(BF16)

    x_srcs = (x.reshape(lat_rows, d), ctx.reshape(batch * ctx_len, d))
    for l in range(depth):
        last = l == depth - 1
        cast_ws = [(w_out, l), (w_ff1, l), (w_ff2, l)] + ([] if last else [(w_in, l + 1)])
        outs = _inproj(
            l, x_srcs, mods, norm1_g[l][None, :], w_in_b,
            jnp.tile(q_norm_g[l], attn_dim // HEAD_DIM)[None, :],
            jnp.tile(k_norm_g[l], KV_DIM // HEAD_DIM)[None, :],
            cos_t, sin_t, s_q, s_k, gmlp_norm_g[l][None, :], w_sp_b[l], b_spatial[l].T, cast_ws,
            lat_tiles=lat_rows // INPROJ_TILE, tiles_per_batch=seq // INPROJ_TILE)
        q, k, vt, gm, w_out_b, w_ff1_b, w_ff2_b = outs[:7]
        outs = outs[7:]
        if not last:
            w_in_b, outs = outs[0], outs[1:]
        xs = outs[0] if len(x_srcs) == 2 else x_srcs[0]
        attn_lat, attn_ctx = _attention(q, k, vt, batch=batch, seq=seq, ctx_len=ctx_len)
        n_tiles = lat_tiles if last else xs.shape[0] // ROW_TILE
        xs = _out_ffn(l, xs, attn_lat, attn_ctx, gm, mods, norm2_g[l][None, :], w_out_b, w_ff1_b, w_ff2_b,
                      n_tiles=n_tiles, lat_tiles=lat_tiles, tiles_per_batch=tiles_per_batch)
        x_srcs = (xs,)
    return xs.reshape(batch, seq, d)
```
